```python
import jax, jax.numpy as jnp
from jax import lax
import numpy as np

D_MODEL = 4096
BATCH = 1
SEQ = 16384
DEPTH = 2
DEC_BATCH = 16
DEC_SEQ = 32
PAST_LEN = 4096

CHUNK = 64
MIX_WIDTH = D_MODEL
HEAD_DIM = 128
A_WIDTH = (3 * MIX_WIDTH) // 8
B_WIDTH = (MIX_WIDTH - A_WIDTH) // 2
C_WIDTH = MIX_WIDTH - A_WIDTH - B_WIDTH
POOL_WINDOWS = (2, 4, 8, 16)
N_POOL_GROUPS = len(POOL_WINDOWS)
POOL_GROUP = C_WIDTH // N_POOL_GROUPS
POOL_HIST = max(POOL_WINDOWS) - 1
CONV_A_WIDTH = 31
CONV_B_WIDTH = 3
IN_COLS = 2 * A_WIDTH + 3 * B_WIDTH + C_WIDTH
D_FF = 4 * D_MODEL
EPS = 1e-6

kernel_name = "hybrid_streaming_conv_pool_encoder_step"


def rmsnorm(x, g):
    xf = x.astype(jnp.float32)
    y = xf * lax.rsqrt(jnp.mean(xf * xf, axis=-1, keepdims=True) + EPS)
    return (y * g.astype(jnp.float32)).astype(x.dtype)


def layernorm(x, g, b):
    xf = x.astype(jnp.float32)
    mu = jnp.mean(xf, axis=-1, keepdims=True)
    xc = xf - mu
    y = xc * lax.rsqrt(jnp.mean(xc * xc, axis=-1, keepdims=True) + EPS)
    return (y * g.astype(jnp.float32) + b.astype(jnp.float32)).astype(x.dtype)


def causal_depthwise(ext, w):
    c = w.shape[1]
    return lax.conv_general_dilated(
        ext, w[:, None, :].astype(ext.dtype), window_strides=(1,), padding='VALID',
        dimension_numbers=('NWC', 'WIO', 'NWC'), feature_group_count=c)


def multiscale_pool(ext, pos):
    H = POOL_HIST
    T = ext.shape[1] - H
    cs = jnp.cumsum(ext.astype(jnp.float32), axis=1)
    cs = jnp.concatenate([jnp.zeros_like(cs[:, :1]), cs], axis=1)
    end = cs[:, H + 1:]
    outs = []
    for g, w in enumerate(POOL_WINDOWS):
        lo, hi = g * POOL_GROUP, (g + 1) * POOL_GROUP
        start = cs[:, H + 1 - w:H + 1 - w + T, lo:hi]
        cnt = jnp.minimum(pos + 1, w).astype(jnp.float32)[None, :, None]
        outs.append((end[..., lo:hi] - start) / cnt)
    pooled = jnp.concatenate(outs, axis=-1)
    return (pooled - ext[:, H:].astype(jnp.float32)).astype(ext.dtype)


def mixer_block(xn, hist_a, hist_b, hist_c, pos, w_in, b_in, a_dw, a_dw_b, a_ln_g, a_ln_b,
                b_dw, c_w, c_scale, w_out):
    Bsz, T, _ = xn.shape
    proj = jnp.einsum('btd,de->bte', xn, w_in) + b_in
    s = np.cumsum([A_WIDTH, A_WIDTH, B_WIDTH, B_WIDTH, B_WIDTH]).tolist()
    a_val, a_gate, b_b, b_c, b_h, c_u = jnp.split(proj, s, axis=-1)
    a_in = a_val * jax.nn.sigmoid(a_gate)
    a_ext = jnp.concatenate([hist_a, a_in], axis=1)
    a_conv = causal_depthwise(a_ext, a_dw) + a_dw_b
    a_out = jax.nn.silu(layernorm(a_conv, a_ln_g, a_ln_b))
    b_x = b_c * b_h
    b_ext = jnp.concatenate([hist_b, b_x], axis=1)
    b_out = b_b * causal_depthwise(b_ext, b_dw)
    c_ext = jnp.concatenate([hist_c, c_u], axis=1)
    c_pool = multiscale_pool(c_ext, pos).reshape(Bsz, T, N_POOL_GROUPS, POOL_GROUP)
    c_out = jnp.einsum('btgi,gio->btgo', c_pool, c_w).reshape(Bsz, T, C_WIDTH) * c_scale
    mix = jnp.concatenate([a_out, b_out, c_out], axis=-1)
    y = jnp.einsum('btm,md->btd', mix, w_out)
    return (y, a_ext[:, -(CONV_A_WIDTH - 1):], b_ext[:, -(CONV_B_WIDTH - 1):],
            c_ext[:, -POOL_HIST:])


def trunk(x, hist_a, hist_b, hist_c, pos, norm_mix, w_in, b_in, a_dw, a_dw_b, a_ln_g, a_ln_b,
          b_dw, c_w, c_scale, w_out, norm_ffn, w_up, w_down, norm_final):
    new_a, new_b, new_c = [], [], []
    for l in range(DEPTH):
        xn = rmsnorm(x, norm_mix[l])
        y, na, nb, nc = mixer_block(xn, hist_a[l], hist_b[l], hist_c[l], pos, w_in[l], b_in[l],
                                    a_dw[l], a_dw_b[l], a_ln_g[l], a_ln_b[l], b_dw[l], c_w[l],
                                    c_scale[l], w_out[l])
        x = x + y
        hn = rmsnorm(x, norm_ffn[l])
        u = jnp.square(jax.nn.relu(jnp.einsum('btd,df->btf', hn, w_up[l])))
        x = x + jnp.einsum('btf,fd->btd', u, w_down[l])
        new_a.append(na)
        new_b.append(nb)
        new_c.append(nc)
    return (rmsnorm(x, norm_final), jnp.stack(new_a, 0), jnp.stack(new_b, 0),
            jnp.stack(new_c, 0))


def setup_inputs(seed: int = 0) -> dict:
    key = jax.random.key(seed)
    ks = jax.random.split(key, 24)
    f32 = jnp.float32
    nrm = lambda k, shape, scale: jax.random.normal(k, shape, f32) * scale
    return {
        "x_prompt": nrm(ks[0], (BATCH, SEQ, D_MODEL), 1.0),
        "x_sample": nrm(ks[1], (DEC_BATCH, DEC_SEQ, D_MODEL), 1.0),
        "state_conv_a": nrm(ks[2], (DEPTH, DEC_BATCH, CONV_A_WIDTH - 1, A_WIDTH), 0.5),
        "state_conv_b": nrm(ks[3], (DEPTH, DEC_BATCH, CONV_B_WIDTH - 1, B_WIDTH), 1.0),
        "state_pool": nrm(ks[4], (DEPTH, DEC_BATCH, POOL_HIST, C_WIDTH), 1.0),
        "norm_mix": 1.0 + nrm(ks[5], (DEPTH, D_MODEL), 0.05),
        "w_in": nrm(ks[6], (DEPTH, D_MODEL, IN_COLS), D_MODEL ** -0.5),
        "b_in": nrm(ks[7], (DEPTH, IN_COLS), 0.02),
        "a_dw": nrm(ks[8], (DEPTH, CONV_A_WIDTH, A_WIDTH), CONV_A_WIDTH ** -0.5),
        "a_dw_b": nrm(ks[9], (DEPTH, A_WIDTH), 0.02),
        "a_ln_g": 1.0 + nrm(ks[10], (DEPTH, A_WIDTH), 0.05),
        "a_ln_b": nrm(ks[11], (DEPTH, A_WIDTH), 0.02),
        "b_dw": nrm(ks[12], (DEPTH, CONV_B_WIDTH, B_WIDTH), CONV_B_WIDTH ** -0.5),
        "c_w": nrm(ks[13], (DEPTH, N_POOL_GROUPS, POOL_GROUP, POOL_GROUP), POOL_GROUP ** -0.5),
        "c_scale": 1.0 + nrm(ks[14], (DEPTH, C_WIDTH), 0.1),
        "w_out": nrm(ks[15], (DEPTH, MIX_WIDTH, D_MODEL), MIX_WIDTH ** -0.5),
        "norm_ffn": 1.0 + nrm(ks[16], (DEPTH, D_MODEL), 0.05),
        "w_up": nrm(ks[17], (DEPTH, D_MODEL, D_FF), D_MODEL ** -0.5),
        "w_down": nrm(ks[18], (DEPTH, D_FF, D_MODEL), 0.5 * D_FF ** -0.5),
        "norm_final": 1.0 + nrm(ks[19], (D_MODEL,), 0.05),
    }


def reference(x_prompt, x_sample, state_conv_a, state_conv_b, state_pool, norm_mix, w_in, b_in,
              a_dw, a_dw_b, a_ln_g, a_ln_b, b_dw, c_w, c_scale, w_out, norm_ffn, w_up, w_down,
              norm_final):
    bp, tp = x_prompt.shape[0], x_prompt.shape[1]
    ts = x_sample.shape[1]
    dt = x_prompt.dtype
    za = jnp.zeros((DEPTH, bp, CONV_A_WIDTH - 1, A_WIDTH), dt)
    zb = jnp.zeros((DEPTH, bp, CONV_B_WIDTH - 1, B_WIDTH), dt)
    zc = jnp.zeros((DEPTH, bp, POOL_HIST, C_WIDTH), dt)
    pos_p = jnp.arange(tp, dtype=jnp.int32)
    pos_s = PAST_LEN + jnp.arange(ts, dtype=jnp.int32)
    y_prompt, na_p, nb_p, nc_p = trunk(x_prompt, za, zb, zc, pos_p, norm_mix, w_in, b_in, a_dw,
                                       a_dw_b, a_ln_g, a_ln_b, b_dw, c_w, c_scale, w_out,
                                       norm_ffn, w_up, w_down, norm_final)
    y_sample, na_s, nb_s, nc_s = trunk(x_sample, state_conv_a.astype(x_sample.dtype),
                                       state_conv_b.astype(x_sample.dtype),
                                       state_pool.astype(x_sample.dtype), pos_s, norm_mix, w_in,
                                       b_in, a_dw, a_dw_b, a_ln_g, a_ln_b, b_dw, c_w, c_scale,
                                       w_out, norm_ffn, w_up, w_down, norm_final)
    return (y_prompt, y_sample, na_p, nb_p, nc_p, na_s, nb_s, nc_s)
```

```python
import functools

import jax
import jax.numpy as jnp
from jax import lax
from jax.experimental import pallas as pl
from jax.experimental.pallas import tpu as pltpu

EPS = 1e-6
PAST_LEN = 4096
POOL_WINDOWS = (2, 4, 8, 16)
CONV_A_TAPS = 31
CONV_B_TAPS = 3
HIST_A = CONV_A_TAPS - 1
HIST_B = CONV_B_TAPS - 1
HIST_C = max(POOL_WINDOWS) - 1

V7X_VMEM_BYTES = 64 * 1024 * 1024
V7X_SUBLANES = 8
V7X_LANES = 128
VMEM_LIMIT_BYTES = V7X_VMEM_BYTES - 8 * 1024 * 1024

EXT_A_PAD = 32
EXT_B_PAD = 8
EXT_C_PAD = 16


def _compiler_params(semantics):
    return pltpu.CompilerParams(dimension_semantics=semantics,
                                vmem_limit_bytes=VMEM_LIMIT_BYTES)


def _rmsnorm_kernel(x_ref, g_ref, o_ref, *, chunk):
    rows = x_ref.shape[0]
    g = g_ref[...]

    def body(c, carry):
        rs = pl.ds(pl.multiple_of(c * chunk, chunk), chunk)
        x = x_ref[rs, :]
        ms = jnp.mean(x * x, axis=-1, keepdims=True)
        o_ref[rs, :] = (x * lax.rsqrt(ms + EPS) * g).astype(o_ref.dtype)
        return carry

    lax.fori_loop(0, rows // chunk, body, 0)


def _rmsnorm(x, g, out_dtype, *, block_rows=512, chunk=16):
    n, d = x.shape
    assert n % block_rows == 0 and block_rows % chunk == 0
    return pl.pallas_call(
        functools.partial(_rmsnorm_kernel, chunk=chunk),
        out_shape=jax.ShapeDtypeStruct((n, d), out_dtype),
        grid=(n // block_rows,),
        in_specs=[pl.BlockSpec((block_rows, d), lambda i: (i, 0)),
                  pl.BlockSpec((1, d), lambda i: (0, 0))],
        out_specs=pl.BlockSpec((block_rows, d), lambda i: (i, 0)),
        compiler_params=_compiler_params(("arbitrary",)),
        name="rmsnorm",
    )(x, g.reshape(1, d))


def _matmul_kernel(*refs, nk, has_bias, act, has_res, sub_rows):
    it = iter(refs)
    a_ref = next(it)
    b_ref = next(it)
    bias_ref = next(it) if has_bias else None
    res_ref = next(it) if has_res else None
    o_ref = next(it)
    acc_ref = next(it) if nk > 1 else None

    def epilogue(acc, rs):
        if has_bias:
            acc = acc + bias_ref[...]
        if act == "relu2":
            r = jnp.maximum(acc, 0.0)
            acc = r * r
        if has_res:
            acc = acc + res_ref[rs, :]
        o_ref[rs, :] = acc.astype(o_ref.dtype)

    tm = a_ref.shape[0]
    row_slices = [slice(r * sub_rows, (r + 1) * sub_rows) for r in range(tm // sub_rows)]
    k = pl.program_id(2)
    if nk > 1:
        @pl.when(k == 0)
        def _():
            acc_ref[...] = jnp.zeros_like(acc_ref)

    for rs in row_slices:
        prod = jnp.dot(a_ref[rs, :], b_ref[...], preferred_element_type=jnp.float32)
        if nk == 1:
            epilogue(prod, rs)
        else:
            acc_ref[rs, :] += prod

    if nk > 1:
        @pl.when(k == nk - 1)
        def _():
            for rs in row_slices:
                epilogue(acc_ref[rs, :], rs)


def _matmul(a, b, *, tm, tn, tk=None, bias=None, act=None, res=None, out_dtype, name,
            sub_rows=256):
    m, kdim = a.shape
    _, n = b.shape
    tk = kdim if tk is None else tk
    assert m % tm == 0 and n % tn == 0 and kdim % tk == 0 and tm % sub_rows == 0
    nk = kdim // tk
    in_specs = [pl.BlockSpec((tm, tk), lambda i, j, k: (i, k)),
                pl.BlockSpec((tk, tn), lambda i, j, k: (k, j))]
    operands = [a, b]
    if bias is not None:
        in_specs.append(pl.BlockSpec((1, tn), lambda i, j, k: (0, j)))
        operands.append(bias.reshape(1, n))
    if res is not None:
        in_specs.append(pl.BlockSpec((tm, tn), lambda i, j, k: (i, j)))
        operands.append(res)
    scratch = [pltpu.VMEM((tm, tn), jnp.float32)] if nk > 1 else []
    return pl.pallas_call(
        functools.partial(_matmul_kernel, nk=nk, has_bias=bias is not None, act=act,
                          has_res=res is not None, sub_rows=sub_rows),
        out_shape=jax.ShapeDtypeStruct((m, n), out_dtype),
        grid=(m // tm, n // tn, nk),
        in_specs=in_specs,
        out_specs=pl.BlockSpec((tm, tn), lambda i, j, k: (i, j)),
        scratch_shapes=scratch,
        compiler_params=_compiler_params(("arbitrary", "arbitrary", "arbitrary")),
        name=name,
    )(*operands)


def _mixer_kernel(proj_ref, ha_ref, hb_ref, hc_ref, adw_ref, adwb_ref, lng_ref, lnb_ref,
                  bdw_ref, cbd_ref, cs_ref,
                  mix_ref, na_ref, nb_ref, nc_ref,
                  ext_a, ext_b, ext_c, conv_a, pool_c,
                  *, rows, carry, pos_base, a_width, b_width, c_width, row_chunk):
    i = pl.program_id(0)
    n_chunks = rows // row_chunk
    off_gate = a_width
    off_bb = 2 * a_width
    off_bc = off_bb + b_width
    off_bh = off_bc + b_width
    off_cu = off_bh + b_width
    pool_group = c_width // len(POOL_WINDOWS)

    def load_history():
        ext_a[0:EXT_A_PAD, :] = jnp.zeros((EXT_A_PAD, a_width), jnp.float32)
        ext_b[0:EXT_B_PAD, :] = jnp.zeros((EXT_B_PAD, b_width), jnp.float32)
        ext_c[0:EXT_C_PAD, :] = jnp.zeros((EXT_C_PAD, c_width), jnp.float32)
        ext_a[EXT_A_PAD - HIST_A:EXT_A_PAD, :] = ha_ref[0]
        ext_b[EXT_B_PAD - HIST_B:EXT_B_PAD, :] = hb_ref[0]
        ext_c[EXT_C_PAD - HIST_C:EXT_C_PAD, :] = hc_ref[0]

    if carry:
        pl.when(i == 0)(load_history)
    else:
        load_history()

    def row_slice(c, base=0):
        return pl.ds(pl.multiple_of(c * row_chunk, row_chunk) + base, row_chunk)

    def stage_inputs(c, carry_):
        rs = row_slice(c)
        a_val = proj_ref[rs, 0:a_width]
        a_gate = proj_ref[rs, off_gate:off_gate + a_width]
        ext_a[row_slice(c, EXT_A_PAD), :] = a_val * jax.nn.sigmoid(a_gate)
        b_c = proj_ref[rs, off_bc:off_bc + b_width]
        b_h = proj_ref[rs, off_bh:off_bh + b_width]
        ext_b[row_slice(c, EXT_B_PAD), :] = b_c * b_h
        ext_c[row_slice(c, EXT_C_PAD), :] = proj_ref[rs, off_cu:off_cu + c_width]
        return carry_

    lax.fori_loop(0, n_chunks, stage_inputs, 0)

    lane_chunk = 2 * V7X_LANES

    def window(ext, c, pad, ls=slice(None)):
        return ext[pl.ds(pl.multiple_of(c * row_chunk, row_chunk), row_chunk + pad), ls]

    def shifted(win, offset):
        n = win.shape[0]
        sub = offset % V7X_SUBLANES
        base = offset - sub
        if sub:
            win = pltpu.roll(win, n - sub, axis=0)
        return win[base:base + row_chunk]

    def conv_a_rows(c, carry_):
        for lc in range(a_width // lane_chunk):
            ls = slice(lc * lane_chunk, (lc + 1) * lane_chunk)
            win = window(ext_a, c, EXT_A_PAD, ls)
            acc = jnp.broadcast_to(adwb_ref[:, ls], (row_chunk, lane_chunk))
            for sub in range(V7X_SUBLANES):
                rot = pltpu.roll(win, win.shape[0] - sub, axis=0) if sub else win
                for k in range(CONV_A_TAPS):
                    offset = EXT_A_PAD - HIST_A + k
                    if offset % V7X_SUBLANES == sub:
                        base = offset - sub
                        acc = acc + rot[base:base + row_chunk] * adw_ref[k:k + 1, ls]
            conv_a[row_slice(c), ls] = acc
        return carry_

    lax.fori_loop(0, n_chunks, conv_a_rows, 0)

    lane = lax.broadcasted_iota(jnp.int32, (row_chunk, c_width), 1)
    row = lax.broadcasted_iota(jnp.int32, (row_chunk, c_width), 0)

    def finish_rows(c, carry_):
        rs = row_slice(c)
        x = conv_a[rs, :]
        mu = jnp.mean(x, axis=-1, keepdims=True)
        xc = x - mu
        var = jnp.mean(xc * xc, axis=-1, keepdims=True)
        y = xc * lax.rsqrt(var + EPS) * lng_ref[...] + lnb_ref[...]
        mix_ref[rs, 0:a_width] = (y * jax.nn.sigmoid(y)).astype(mix_ref.dtype)

        win_b = window(ext_b, c, EXT_B_PAD)
        conv_b = shifted(win_b, EXT_B_PAD - HIST_B) * bdw_ref[0:1, :]
        for k in range(1, CONV_B_TAPS):
            conv_b = conv_b + shifted(win_b, EXT_B_PAD - HIST_B + k) * bdw_ref[k:k + 1, :]
        b_b = proj_ref[rs, off_bb:off_bb + b_width]
        mix_ref[rs, a_width:a_width + b_width] = (b_b * conv_b).astype(mix_ref.dtype)

        win_c = window(ext_c, c, EXT_C_PAD)
        n_win = win_c.shape[0]
        cur = win_c[EXT_C_PAD:]
        pos = pos_base + i * rows + c * row_chunk + row
        wsum = win_c
        pooled = jnp.zeros_like(cur)
        span = 1
        for g, w in enumerate(POOL_WINDOWS):
            while span < w:
                wsum = wsum + pltpu.roll(wsum, span, axis=0)
                span *= 2
            cnt = jnp.minimum(pos + 1, w).astype(jnp.float32)
            in_group = (lane >= g * pool_group) & (lane < (g + 1) * pool_group)
            pooled = jnp.where(in_group, wsum[EXT_C_PAD:] / cnt, pooled)
        pool_c[rs, :] = (pooled - cur).astype(pool_c.dtype)
        return carry_

    lax.fori_loop(0, n_chunks, finish_rows, 0)

    c_lin = jnp.dot(pool_c[...], cbd_ref[...], preferred_element_type=jnp.float32)
    mix_ref[:, a_width + b_width:] = (c_lin * cs_ref[...]).astype(mix_ref.dtype)

    na_ref[0] = ext_a[rows + EXT_A_PAD - HIST_A:rows + EXT_A_PAD, :]
    nb_ref[0] = ext_b[rows + EXT_B_PAD - HIST_B:rows + EXT_B_PAD, :]
    nc_ref[0] = ext_c[rows + EXT_C_PAD - HIST_C:rows + EXT_C_PAD, :]

    if carry:
        ext_a[0:EXT_A_PAD, :] = ext_a[rows:rows + EXT_A_PAD, :]
        ext_b[0:EXT_B_PAD, :] = ext_b[rows:rows + EXT_B_PAD, :]
        ext_c[0:EXT_C_PAD, :] = ext_c[rows:rows + EXT_C_PAD, :]


def _mixer(proj, hist_a, hist_b, hist_c, a_dw, a_dw_b, a_ln_g, a_ln_b, b_dw, c_bd, c_scale,
           *, row_block0, n_steps, rows, carry, pos_base, name):
    a_width = a_dw.shape[1]
    b_width = b_dw.shape[1]
    c_width = c_scale.shape[0]
    in_cols = proj.shape[1]
    mix_width = a_width + b_width + c_width
    n_streams = hist_a.shape[0]
    row_chunk = min(rows, 32)

    def const(shape):
        return pl.BlockSpec(shape, lambda s: tuple(0 for _ in shape))

    def per_stream(shape):
        if carry:
            return pl.BlockSpec((1,) + shape, lambda s: (0, 0, 0))
        return pl.BlockSpec((1,) + shape, lambda s: (s, 0, 0))

    kern = functools.partial(_mixer_kernel, rows=rows, carry=carry, pos_base=pos_base,
                             a_width=a_width, b_width=b_width, c_width=c_width,
                             row_chunk=row_chunk)
    return pl.pallas_call(
        kern,
        out_shape=(jax.ShapeDtypeStruct((n_steps * rows, mix_width), jnp.bfloat16),
                   jax.ShapeDtypeStruct((n_streams, HIST_A, a_width), jnp.float32),
                   jax.ShapeDtypeStruct((n_streams, HIST_B, b_width), jnp.float32),
                   jax.ShapeDtypeStruct((n_streams, HIST_C, c_width), jnp.float32)),
        grid=(n_steps,),
        in_specs=[pl.BlockSpec((rows, in_cols), lambda s: (row_block0 + s, 0)),
                  per_stream((HIST_A, a_width)),
                  per_stream((HIST_B, b_width)),
                  per_stream((HIST_C, c_width)),
                  const((CONV_A_TAPS, a_width)),
                  const((1, a_width)), const((1, a_width)), const((1, a_width)),
                  const((CONV_B_TAPS, b_width)),
                  const((c_width, c_width)),
                  const((1, c_width))],
        out_specs=(pl.BlockSpec((rows, mix_width), lambda s: (s, 0)),
                   per_stream((HIST_A, a_width)),
                   per_stream((HIST_B, b_width)),
                   per_stream((HIST_C, c_width))),
        scratch_shapes=[pltpu.VMEM((rows + EXT_A_PAD, a_width), jnp.float32),
                        pltpu.VMEM((rows + EXT_B_PAD, b_width), jnp.float32),
                        pltpu.VMEM((rows + EXT_C_PAD, c_width), jnp.float32),
                        pltpu.VMEM((rows, a_width), jnp.float32),
                        pltpu.VMEM((rows, c_width), jnp.bfloat16)],
        compiler_params=_compiler_params(("arbitrary",)),
        name=name,
    )(proj, hist_a, hist_b, hist_c, a_dw, a_dw_b.reshape(1, -1), a_ln_g.reshape(1, -1),
      a_ln_b.reshape(1, -1), b_dw, c_bd, c_scale.reshape(1, -1))


def _block_diag(c_w):
    g, n, _ = c_w.shape
    eye = jnp.eye(g, dtype=c_w.dtype)
    return (eye[:, None, :, None] * c_w[:, :, None, :]).reshape(g * n, g * n)


def kernel(x_prompt, x_sample, state_conv_a, state_conv_b, state_pool, norm_mix, w_in, b_in,
           a_dw, a_dw_b, a_ln_g, a_ln_b, b_dw, c_w, c_scale, w_out, norm_ffn, w_up, w_down,
           norm_final):
    depth = w_in.shape[0]
    bp, tp, d_model = x_prompt.shape
    bs, ts, _ = x_sample.shape
    assert bp == 1, "prompt rows must form one stream"
    n_prompt = bp * tp
    n_sample = bs * ts
    n_rows = n_prompt + n_sample
    bf16 = jnp.bfloat16
    f32 = jnp.float32

    x = jnp.concatenate([x_prompt.reshape(n_prompt, d_model),
                         x_sample.reshape(n_sample, d_model)], axis=0)

    w_in_b = w_in.astype(bf16)
    w_out_b = w_out.astype(bf16)
    w_up_b = w_up.astype(bf16)
    w_down_b = w_down.astype(bf16)

    tm = 1536
    prompt_rows = 256
    assert n_rows % tm == 0 and n_prompt % prompt_rows == 0 and n_prompt % ts == 0

    new_a_p, new_b_p, new_c_p, new_a_s, new_b_s, new_c_s = [], [], [], [], [], []
    for l in range(depth):
        xn = _rmsnorm(x, norm_mix[l], bf16)
        proj = _matmul(xn, w_in_b[l], tm=tm, tn=1024, bias=b_in[l], out_dtype=f32,
                       name="in_proj")
        c_bd = _block_diag(c_w[l]).astype(bf16)
        params = (a_dw[l], a_dw_b[l], a_ln_g[l], a_ln_b[l], b_dw[l], c_bd, c_scale[l])
        zeros = lambda h, w: jnp.zeros((bp, h, w), f32)
        mix_p, na_p, nb_p, nc_p = _mixer(
            proj, zeros(HIST_A, a_dw.shape[2]), zeros(HIST_B, b_dw.shape[2]),
            zeros(HIST_C, c_scale.shape[1]), *params,
            row_block0=0, n_steps=n_prompt // prompt_rows, rows=prompt_rows, carry=True,
            pos_base=0, name="mixer_prompt")
        mix_s, na_s, nb_s, nc_s = _mixer(
            proj, state_conv_a[l].astype(f32), state_conv_b[l].astype(f32),
            state_pool[l].astype(f32), *params,
            row_block0=n_prompt // ts, n_steps=bs, rows=ts, carry=False,
            pos_base=PAST_LEN, name="mixer_sample")
        mix = jnp.concatenate([mix_p, mix_s], axis=0)
        x = _matmul(mix, w_out_b[l], tm=tm, tn=512, res=x, out_dtype=f32, name="out_proj")
        hn = _rmsnorm(x, norm_ffn[l], bf16)
        u = _matmul(hn, w_up_b[l], tm=tm, tn=1024, act="relu2", out_dtype=bf16, name="ffn_up")
        x = _matmul(u, w_down_b[l], tm=tm, tn=1024, tk=2048, res=x, out_dtype=f32,
                    name="ffn_down")
        new_a_p.append(na_p)
        new_b_p.append(nb_p)
        new_c_p.append(nc_p)
        new_a_s.append(na_s)
        new_b_s.append(nb_s)
        new_c_s.append(nc_s)

    y = _rmsnorm(x, norm_final, x_prompt.dtype)
    y_prompt = y[:n_prompt].reshape(bp, tp, d_model)
    y_sample = y[n_prompt:].reshape(bs, ts, d_model)
    return (y_prompt, y_sample, jnp.stack(new_a_p, 0), jnp.stack(new_b_p, 0),
            jnp.stack(new_c_p, 0), jnp.stack(new_a_s, 0), jnp.stack(new_b_s, 0),
            jnp.stack(new_c_s, 0))
```

```python
import functools

import jax
import jax.numpy as jnp
from jax import lax
from jax.experimental import pallas as pl
from jax.experimental.pallas import tpu as pltpu

EPS = 1e-6
PAST_LEN = 4096
POOL_WINDOWS = (2, 4, 8, 16)
CONV_A_TAPS = 31
CONV_B_TAPS = 3
HIST_A = CONV_A_TAPS - 1
HIST_B = CONV_B_TAPS - 1
HIST_C = max(POOL_WINDOWS) - 1

V7X_VMEM_BYTES = 64 * 1024 * 1024
V7X_SUBLANES = 8
V7X_LANES = 128
VMEM_LIMIT_BYTES = V7X_VMEM_BYTES - 4 * 1024 * 1024

EXT_A_PAD = 32
EXT_B_PAD = 8
EXT_C_PAD = 16
MIXER_ROW_CHUNK = 32
CONV_A_ROW_CHUNK = 64

assert all(w & (w - 1) == 0 for w in POOL_WINDOWS) and list(POOL_WINDOWS) == sorted(POOL_WINDOWS)


def _compiler_params(semantics):
    return pltpu.CompilerParams(dimension_semantics=semantics,
                                vmem_limit_bytes=VMEM_LIMIT_BYTES)


def _embed_kernel(xp_ref, xs_ref, g_ref, x_ref, xg_ref, ssq_ref, *, n_prompt_blocks, chunk):
    i = pl.program_id(0)
    rows = x_ref.shape[0]
    g = g_ref[...]

    def copy_from(src_ref):
        def body(c, carry):
            rs = pl.ds(pl.multiple_of(c * chunk, chunk), chunk)
            x = src_ref[rs, :]
            x_ref[rs, :] = x
            xg_ref[rs, :] = (x * g).astype(xg_ref.dtype)
            ssq_ref[rs, :] = jnp.sum(x * x, axis=-1, keepdims=True)
            return carry
        lax.fori_loop(0, rows // chunk, body, 0)

    pl.when(i < n_prompt_blocks)(lambda: copy_from(xp_ref))
    pl.when(i >= n_prompt_blocks)(lambda: copy_from(xs_ref))


def _embed(x_prompt, x_sample, g, *, chunk=16):
    n_prompt, d = x_prompt.shape
    block_rows = x_sample.shape[0]
    assert n_prompt % block_rows == 0 and block_rows % chunk == 0
    n_prompt_blocks = n_prompt // block_rows
    n = n_prompt + block_rows
    row_block = lambda i: (i, 0)
    return pl.pallas_call(
        functools.partial(_embed_kernel, n_prompt_blocks=n_prompt_blocks, chunk=chunk),
        out_shape=(jax.ShapeDtypeStruct((n, d), jnp.float32),
                   jax.ShapeDtypeStruct((n, d), jnp.bfloat16),
                   jax.ShapeDtypeStruct((n, 1), jnp.float32)),
        grid=(n_prompt_blocks + 1,),
        in_specs=[pl.BlockSpec((block_rows, d), lambda i: (jnp.minimum(i, n_prompt_blocks - 1), 0)),
                  pl.BlockSpec((block_rows, d), lambda i: (0, 0)),
                  pl.BlockSpec((1, d), lambda i: (0, 0))],
        out_specs=(pl.BlockSpec((block_rows, d), row_block),
                   pl.BlockSpec((block_rows, d), row_block),
                   pl.BlockSpec((block_rows, 1), row_block)),
        compiler_params=_compiler_params(("arbitrary",)),
        name="embed",
    )(x_prompt, x_sample, g.reshape(1, d))


def _final_norm_kernel(x_ref, ssq_ref, g_ref, yp_ref, ys_ref, *, n_prompt_blocks, chunk):
    i = pl.program_id(0)
    rows, d = x_ref.shape
    g = g_ref[...]

    def write_to(dst_ref):
        def body(c, carry):
            rs = pl.ds(pl.multiple_of(c * chunk, chunk), chunk)
            r = lax.rsqrt(ssq_ref[rs, :] * (1.0 / d) + EPS)
            dst_ref[rs, :] = (x_ref[rs, :] * r * g).astype(dst_ref.dtype)
            return carry
        lax.fori_loop(0, rows // chunk, body, 0)

    pl.when(i < n_prompt_blocks)(lambda: write_to(yp_ref))
    pl.when(i >= n_prompt_blocks)(lambda: write_to(ys_ref))


def _final_norm(x, ssq, g, *, n_prompt, out_dtype, chunk=16):
    n, d = x.shape
    block_rows = n - n_prompt
    assert n_prompt % block_rows == 0 and block_rows % chunk == 0
    n_prompt_blocks = n_prompt // block_rows
    row_block = lambda i: (i, 0)
    return pl.pallas_call(
        functools.partial(_final_norm_kernel, n_prompt_blocks=n_prompt_blocks, chunk=chunk),
        out_shape=(jax.ShapeDtypeStruct((n_prompt, d), out_dtype),
                   jax.ShapeDtypeStruct((block_rows, d), out_dtype)),
        grid=(n_prompt_blocks + 1,),
        in_specs=[pl.BlockSpec((block_rows, d), row_block),
                  pl.BlockSpec((block_rows, 1), row_block),
                  pl.BlockSpec((1, d), lambda i: (0, 0))],
        out_specs=(pl.BlockSpec((block_rows, d), lambda i: (jnp.minimum(i, n_prompt_blocks - 1), 0)),
                   pl.BlockSpec((block_rows, d), lambda i: (0, 0))),
        compiler_params=_compiler_params(("arbitrary",)),
        name="final_norm",
    )(x, ssq, g.reshape(1, d))


def _matmul_kernel(*refs, nk, norm_dim, has_bias, act, has_res, emit_xg, emit_ssq, sub_rows):
    it = iter(refs)
    a_ref = next(it)
    b_ref = next(it)
    ssq_in_ref = next(it) if norm_dim else None
    bias_ref = next(it) if has_bias else None
    res_ref = next(it) if has_res else None
    gain_ref = next(it) if emit_xg else None
    o_ref = next(it)
    xg_ref = next(it) if emit_xg else None
    ssq_out_ref = next(it) if emit_ssq else None
    acc_ref = next(it) if nk > 1 else None

    j = pl.program_id(1)
    k = pl.program_id(2)
    tm = a_ref.shape[0]
    row_slices = [slice(r * sub_rows, (r + 1) * sub_rows) for r in range(tm // sub_rows)]

    if emit_ssq:
        @pl.when((j == 0) & (k == 0))
        def _():
            ssq_out_ref[...] = jnp.zeros_like(ssq_out_ref)

    def epilogue(acc, rs):
        if norm_dim:
            acc = acc * lax.rsqrt(ssq_in_ref[rs, :] * (1.0 / norm_dim) + EPS)
        if has_bias:
            acc = acc + bias_ref[...]
        if act == "relu2":
            r = jnp.maximum(acc, 0.0)
            acc = r * r
        if has_res:
            acc = acc + res_ref[rs, :]
        o_ref[rs, :] = acc.astype(o_ref.dtype)
        if emit_xg:
            xg_ref[rs, :] = (acc * gain_ref[...]).astype(xg_ref.dtype)
        if emit_ssq:
            ssq_out_ref[rs, :] += jnp.sum(acc * acc, axis=-1, keepdims=True)

    def run(first, last):
        for rs in row_slices:
            prod = jnp.dot(a_ref[rs, :], b_ref[...], preferred_element_type=jnp.float32)
            if not first:
                prod = prod + acc_ref[rs, :]
            if last:
                epilogue(prod, rs)
            else:
                acc_ref[rs, :] = prod

    if nk == 1:
        run(True, True)
    else:
        pl.when(k == 0)(lambda: run(True, False))
        pl.when((k > 0) & (k < nk - 1))(lambda: run(False, False))
        pl.when(k == nk - 1)(lambda: run(False, True))


def _matmul(a, b, *, tm, tn, tk=None, ssq=None, bias=None, act=None, res=None, gain=None,
            emit_ssq=False, out_dtype, name, sub_rows=256):
    m, kdim = a.shape
    _, n = b.shape
    tk = kdim if tk is None else tk
    assert m % tm == 0 and n % tn == 0 and kdim % tk == 0 and tm % sub_rows == 0
    nk = kdim // tk
    assert nk == 1 or nk >= 3
    in_specs = [pl.BlockSpec((tm, tk), lambda i, j, k: (i, k)),
                pl.BlockSpec((tk, tn), lambda i, j, k: (k, j))]
    operands = [a, b]
    if ssq is not None:
        in_specs.append(pl.BlockSpec((tm, 1), lambda i, j, k: (i, 0)))
        operands.append(ssq)
    if bias is not None:
        in_specs.append(pl.BlockSpec((1, tn), lambda i, j, k: (0, j)))
        operands.append(bias.reshape(1, n))
    if res is not None:
        in_specs.append(pl.BlockSpec((tm, tn), lambda i, j, k: (i, j)))
        operands.append(res)
    out_shape = [jax.ShapeDtypeStruct((m, n), out_dtype)]
    out_specs = [pl.BlockSpec((tm, tn), lambda i, j, k: (i, j))]
    if gain is not None:
        in_specs.append(pl.BlockSpec((1, tn), lambda i, j, k: (0, j)))
        operands.append(gain.reshape(1, n))
        out_shape.append(jax.ShapeDtypeStruct((m, n), jnp.bfloat16))
        out_specs.append(pl.BlockSpec((tm, tn), lambda i, j, k: (i, j)))
    if emit_ssq:
        out_shape.append(jax.ShapeDtypeStruct((m, 1), jnp.float32))
        out_specs.append(pl.BlockSpec((tm, 1), lambda i, j, k: (i, 0)))
    scratch = [pltpu.VMEM((tm, tn), jnp.float32)] if nk > 1 else []
    out = pl.pallas_call(
        functools.partial(_matmul_kernel, nk=nk, norm_dim=kdim if ssq is not None else 0,
                          has_bias=bias is not None, act=act, has_res=res is not None,
                          emit_xg=gain is not None, emit_ssq=emit_ssq, sub_rows=sub_rows),
        out_shape=tuple(out_shape),
        grid=(m // tm, n // tn, nk),
        in_specs=in_specs,
        out_specs=tuple(out_specs),
        scratch_shapes=scratch,
        compiler_params=_compiler_params(("arbitrary", "arbitrary", "arbitrary")),
        name=name,
    )(*operands)
    return out if len(out) > 1 else out[0]


def _mixer_kernel(proj_ref, sa_ref, sb_ref, sc_ref, adw_ref, adwb_ref, lng_ref, lnb_ref,
                  bdw_ref, cbd_ref, cs_ref,
                  mix_ref, nap_ref, nbp_ref, ncp_ref, nas_ref, nbs_ref, ncs_ref,
                  ext_a, ext_b, ext_c, conv_a, pool_c,
                  *, n_prompt_steps, sample_rows, a_width, b_width, c_width):
    i = pl.program_id(0)
    block_rows = proj_ref.shape[0]
    row_chunk = MIXER_ROW_CHUNK
    off_gate = a_width
    off_bb = 2 * a_width
    off_bc = off_bb + b_width
    off_bh = off_bc + b_width
    off_cu = off_bh + b_width
    pool_group = c_width // len(POOL_WINDOWS)
    lane = lax.broadcasted_iota(jnp.int32, (row_chunk, V7X_LANES), 1)
    row = lax.broadcasted_iota(jnp.int32, (row_chunk, 1), 0)

    def aligned(start):
        return pl.multiple_of(start, row_chunk)

    def segment(row0, rows, pos0):
        n_chunks = rows // row_chunk

        def block_rows_at(c):
            return pl.ds(aligned(row0 + c * row_chunk), row_chunk)

        def ext_rows_at(c, pad):
            return pl.ds(aligned(c * row_chunk) + pad, row_chunk)

        def window(ext, c, pad, ls=slice(None)):
            return ext[pl.ds(aligned(c * row_chunk), row_chunk + pad), ls]

        def shifted(win, offset):
            sub = offset % V7X_SUBLANES
            base = offset - sub
            if sub:
                win = pltpu.roll(win, win.shape[0] - sub, axis=0)
            return win[base:base + row_chunk]

        def stage_inputs(c, carry_):
            rs = block_rows_at(c)
            a_val = proj_ref[rs, 0:a_width]
            a_gate = proj_ref[rs, off_gate:off_gate + a_width]
            ext_a[ext_rows_at(c, EXT_A_PAD), :] = a_val * jax.nn.sigmoid(a_gate)
            b_c = proj_ref[rs, off_bc:off_bc + b_width]
            b_h = proj_ref[rs, off_bh:off_bh + b_width]
            ext_b[ext_rows_at(c, EXT_B_PAD), :] = b_c * b_h
            ext_c[ext_rows_at(c, EXT_C_PAD), :] = proj_ref[rs, off_cu:off_cu + c_width]
            return carry_

        lax.fori_loop(0, n_chunks, stage_inputs, 0)

        conv_chunk = min(rows, CONV_A_ROW_CHUNK)

        def conv_a_rows(c, carry_):
            for lc in range(a_width // V7X_LANES):
                ls = slice(lc * V7X_LANES, (lc + 1) * V7X_LANES)
                win = ext_a[pl.ds(aligned(c * conv_chunk), conv_chunk + EXT_A_PAD), ls]
                acc = jnp.broadcast_to(adwb_ref[:, ls], (conv_chunk, V7X_LANES))
                for sub in range(V7X_SUBLANES):
                    rot = pltpu.roll(win, win.shape[0] - sub, axis=0) if sub else win
                    for t in range(CONV_A_TAPS):
                        offset = EXT_A_PAD - HIST_A + t
                        if offset % V7X_SUBLANES == sub:
                            base = offset - sub
                            acc = acc + rot[base:base + conv_chunk] * adw_ref[t:t + 1, ls]
                conv_a[pl.ds(aligned(row0 + c * conv_chunk), conv_chunk), ls] = acc
            return carry_

        lax.fori_loop(0, rows // conv_chunk, conv_a_rows, 0)

        def finish_rows(c, carry_):
            rs = block_rows_at(c)
            x = conv_a[rs, :]
            mu = jnp.mean(x, axis=-1, keepdims=True)
            xc = x - mu
            var = jnp.mean(xc * xc, axis=-1, keepdims=True)
            y = xc * lax.rsqrt(var + EPS) * lng_ref[...] + lnb_ref[...]
            mix_ref[rs, 0:a_width] = (y * jax.nn.sigmoid(y)).astype(mix_ref.dtype)

            win_b = window(ext_b, c, EXT_B_PAD)
            conv_b = shifted(win_b, EXT_B_PAD - HIST_B) * bdw_ref[0:1, :]
            for t in range(1, CONV_B_TAPS):
                conv_b = conv_b + shifted(win_b, EXT_B_PAD - HIST_B + t) * bdw_ref[t:t + 1, :]
            b_b = proj_ref[rs, off_bb:off_bb + b_width]
            mix_ref[rs, a_width:a_width + b_width] = (b_b * conv_b).astype(mix_ref.dtype)

            win_c = window(ext_c, c, EXT_C_PAD)
            cur = win_c[EXT_C_PAD:]
            pos = pos0 + c * row_chunk + row
            wsum, col0, span = win_c, 0, 1
            means = []
            for g, w in enumerate(POOL_WINDOWS):
                start = (g * pool_group) // V7X_LANES * V7X_LANES
                wsum, col0 = wsum[:, start - col0:], start
                while span < w:
                    wsum = wsum + pltpu.roll(wsum, span, axis=0)
                    span *= 2
                inv_cnt = 1.0 / jnp.minimum(pos + 1, w).astype(jnp.float32)
                means.append((wsum[EXT_C_PAD:] * inv_cnt, col0))
            tiles = []
            for lo in range(0, c_width, V7X_LANES):
                g_lo = lo // pool_group
                g_hi = (lo + V7X_LANES - 1) // pool_group
                tile_of = lambda g: means[g][0][:, lo - means[g][1]:lo - means[g][1] + V7X_LANES]
                if g_lo == g_hi:
                    tiles.append(tile_of(g_lo))
                else:
                    tiles.append(jnp.where(lane < g_hi * pool_group - lo, tile_of(g_lo),
                                           tile_of(g_hi)))
            pooled = jnp.concatenate(tiles, axis=-1)
            pool_c[rs, :] = (pooled - cur).astype(pool_c.dtype)
            return carry_

        lax.fori_loop(0, n_chunks, finish_rows, 0, unroll=2 if n_chunks % 2 == 0 else 1)

    def load_history(hist_a, hist_b, hist_c):
        ext_a[0:EXT_A_PAD, :] = jnp.zeros((EXT_A_PAD, a_width), jnp.float32)
        ext_b[0:EXT_B_PAD, :] = jnp.zeros((EXT_B_PAD, b_width), jnp.float32)
        ext_c[0:EXT_C_PAD, :] = jnp.zeros((EXT_C_PAD, c_width), jnp.float32)
        if hist_a is not None:
            ext_a[EXT_A_PAD - HIST_A:EXT_A_PAD, :] = hist_a
            ext_b[EXT_B_PAD - HIST_B:EXT_B_PAD, :] = hist_b
            ext_c[EXT_C_PAD - HIST_C:EXT_C_PAD, :] = hist_c

    def new_state(rows):
        return (ext_a[rows + EXT_A_PAD - HIST_A:rows + EXT_A_PAD, :],
                ext_b[rows + EXT_B_PAD - HIST_B:rows + EXT_B_PAD, :],
                ext_c[rows + EXT_C_PAD - HIST_C:rows + EXT_C_PAD, :])

    @pl.when(i < n_prompt_steps)
    def _prompt_step():
        pl.when(i == 0)(lambda: load_history(None, None, None))
        segment(0, block_rows, i * block_rows)
        nap_ref[0], nbp_ref[0], ncp_ref[0] = new_state(block_rows)
        ext_a[0:EXT_A_PAD, :] = ext_a[block_rows:block_rows + EXT_A_PAD, :]
        ext_b[0:EXT_B_PAD, :] = ext_b[block_rows:block_rows + EXT_B_PAD, :]
        ext_c[0:EXT_C_PAD, :] = ext_c[block_rows:block_rows + EXT_C_PAD, :]

    @pl.when(i >= n_prompt_steps)
    def _sample_step():
        def one_stream(s, carry_):
            load_history(sa_ref[s], sb_ref[s], sc_ref[s])
            segment(s * sample_rows, sample_rows, PAST_LEN)
            nas_ref[s], nbs_ref[s], ncs_ref[s] = new_state(sample_rows)
            return carry_
        lax.fori_loop(0, block_rows // sample_rows, one_stream, 0)

    c_lin = jnp.dot(pool_c[...], cbd_ref[...], preferred_element_type=jnp.float32)
    mix_ref[:, a_width + b_width:] = (c_lin * cs_ref[...]).astype(mix_ref.dtype)


def _mixer(proj, state_a, state_b, state_c, a_dw, a_dw_b, a_ln_g, a_ln_b, b_dw, c_bd, c_scale,
           *, n_prompt, sample_rows, block_rows=256):
    n_rows, in_cols = proj.shape
    a_width = a_dw.shape[1]
    b_width = b_dw.shape[1]
    c_width = c_scale.shape[0]
    mix_width = a_width + b_width + c_width
    n_streams = state_a.shape[0]
    streams_per_step = block_rows // sample_rows
    assert n_prompt % block_rows == 0 and block_rows % sample_rows == 0
    assert sample_rows % MIXER_ROW_CHUNK == 0 and sample_rows >= HIST_A
    assert n_rows == n_prompt + n_streams * sample_rows and n_streams % streams_per_step == 0
    n_prompt_steps = n_prompt // block_rows
    n_steps = n_prompt_steps + n_streams // streams_per_step

    def const(shape):
        return pl.BlockSpec(shape, lambda s: tuple(0 for _ in shape))

    def sample_state(h, w):
        return pl.BlockSpec((streams_per_step, h, w),
                            lambda s: (jnp.maximum(s - n_prompt_steps, 0), 0, 0))

    f32 = jnp.float32
    kern = functools.partial(_mixer_kernel, n_prompt_steps=n_prompt_steps,
                             sample_rows=sample_rows, a_width=a_width, b_width=b_width,
                             c_width=c_width)
    return pl.pallas_call(
        kern,
        out_shape=(jax.ShapeDtypeStruct((n_rows, mix_width), jnp.bfloat16),
                   jax.ShapeDtypeStruct((1, HIST_A, a_width), f32),
                   jax.ShapeDtypeStruct((1, HIST_B, b_width), f32),
                   jax.ShapeDtypeStruct((1, HIST_C, c_width), f32),
                   jax.ShapeDtypeStruct((n_streams, HIST_A, a_width), f32),
                   jax.ShapeDtypeStruct((n_streams, HIST_B, b_width), f32),
                   jax.ShapeDtypeStruct((n_streams, HIST_C, c_width), f32)),
        grid=(n_steps,),
        in_specs=[pl.BlockSpec((block_rows, in_cols), lambda s: (s, 0)),
                  sample_state(HIST_A, a_width),
                  sample_state(HIST_B, b_width),
                  sample_state(HIST_C, c_width),
                  const((CONV_A_TAPS, a_width)),
                  const((1, a_width)), const((1, a_width)), const((1, a_width)),
                  const((CONV_B_TAPS, b_width)),
                  const((c_width, c_width)),
                  const((1, c_width))],
        out_specs=(pl.BlockSpec((block_rows, mix_width), lambda s: (s, 0)),
                   const((1, HIST_A, a_width)),
                   const((1, HIST_B, b_width)),
                   const((1, HIST_C, c_width)),
                   sample_state(HIST_A, a_width),
                   sample_state(HIST_B, b_width),
                   sample_state(HIST_C, c_width)),
        scratch_shapes=[pltpu.VMEM((block_rows + EXT_A_PAD, a_width), f32),
                        pltpu.VMEM((block_rows + EXT_B_PAD, b_width), f32),
                        pltpu.VMEM((block_rows + EXT_C_PAD, c_width), f32),
                        pltpu.VMEM((block_rows, a_width), f32),
                        pltpu.VMEM((block_rows, c_width), jnp.bfloat16)],
        compiler_params=_compiler_params(("arbitrary",)),
        name="mixer",
    )(proj, state_a, state_b, state_c, a_dw, a_dw_b.reshape(1, -1), a_ln_g.reshape(1, -1),
      a_ln_b.reshape(1, -1), b_dw, c_bd, c_scale.reshape(1, -1))


def _block_diag(c_w):
    g, n, _ = c_w.shape
    eye = jnp.eye(g, dtype=c_w.dtype)
    return (eye[:, None, :, None] * c_w[:, :, None, :]).reshape(g * n, g * n)


def kernel(x_prompt, x_sample, state_conv_a, state_conv_b, state_pool, norm_mix, w_in, b_in,
           a_dw, a_dw_b, a_ln_g, a_ln_b, b_dw, c_w, c_scale, w_out, norm_ffn, w_up, w_down,
           norm_final):
    depth = w_in.shape[0]
    bp, tp, d_model = x_prompt.shape
    bs, ts, _ = x_sample.shape
    assert bp == 1, "prompt rows must form one stream"
    n_prompt = bp * tp
    n_sample = bs * ts
    bf16 = jnp.bfloat16
    f32 = jnp.float32

    w_in_b = w_in.astype(bf16)
    w_out_b = w_out.astype(bf16)
    w_up_b = w_up.astype(bf16)
    w_down_b = w_down.astype(bf16)

    tm = 1536
    assert (n_prompt + n_sample) % tm == 0

    x, xg, ssq = _embed(x_prompt.reshape(n_prompt, d_model).astype(f32),
                        x_sample.reshape(n_sample, d_model).astype(f32), norm_mix[0])
    new_p = ([], [], [])
    new_s = ([], [], [])
    for l in range(depth):
        proj = _matmul(xg, w_in_b[l], tm=tm, tn=1024, ssq=ssq, bias=b_in[l], out_dtype=f32,
                       name="in_proj")
        c_bd = _block_diag(c_w[l]).astype(bf16)
        mix, na_p, nb_p, nc_p, na_s, nb_s, nc_s = _mixer(
            proj, state_conv_a[l].astype(f32), state_conv_b[l].astype(f32),
            state_pool[l].astype(f32), a_dw[l], a_dw_b[l], a_ln_g[l], a_ln_b[l], b_dw[l], c_bd,
            c_scale[l], n_prompt=n_prompt, sample_rows=ts)
        x, hg, ssq = _matmul(mix, w_out_b[l], tm=tm, tn=512, res=x, gain=norm_ffn[l],
                             emit_ssq=True, out_dtype=f32, name="out_proj")
        u = _matmul(hg, w_up_b[l], tm=tm, tn=1024, ssq=ssq, act="relu2", out_dtype=bf16,
                    name="ffn_up")
        if l + 1 < depth:
            x, xg, ssq = _matmul(u, w_down_b[l], tm=tm, tn=1024, tk=2048, res=x,
                                 gain=norm_mix[l + 1], emit_ssq=True, out_dtype=f32,
                                 name="ffn_down")
        else:
            x, ssq = _matmul(u, w_down_b[l], tm=tm, tn=1024, tk=2048, res=x, emit_ssq=True,
                             out_dtype=f32, name="ffn_down_last")
        for dst, val in zip(new_p + new_s, (na_p, nb_p, nc_p, na_s, nb_s, nc_s)):
            dst.append(val)

    y_prompt, y_sample = _final_norm(x, ssq, norm_final, n_prompt=n_prompt,
                                     out_dtype=x_prompt.dtype)
    stack = lambda parts: jnp.stack(parts, 0)
    return (y_prompt.reshape(bp, tp, d_model), y_sample.reshape(bs, ts, d_model),
            stack(new_p[0]), stack(new_p[1]), stack(new_p[2]),
            stack(new_s[0]), stack(new_s[1]), stack(new_s[2]))
```

```python
import functools

import jax
import jax.numpy as jnp
from jax import lax
from jax.experimental import pallas as pl
from jax.experimental.pallas import tpu as pltpu

EPS = 1e-6
PAST_LEN = 4096
POOL_WINDOWS = (2, 4, 8, 16)
CONV_A_TAPS = 31
CONV_B_TAPS = 3
HIST_A = CONV_A_TAPS - 1
HIST_B = CONV_B_TAPS - 1
HIST_C = max(POOL_WINDOWS) - 1

V7X_VMEM_BYTES = 64 * 1024 * 1024
V7X_SUBLANES = 8
V7X_LANES = 128
VMEM_LIMIT_BYTES = V7X_VMEM_BYTES - 4 * 1024 * 1024

EXT_A_PAD = 32
EXT_B_PAD = 8
EXT_C_PAD = 16
MIXER_ROW_CHUNK = 32
CONV_A_ROW_CHUNK = 64

assert all(w & (w - 1) == 0 for w in POOL_WINDOWS) and list(POOL_WINDOWS) == sorted(POOL_WINDOWS)


def _compiler_params(semantics):
    return pltpu.CompilerParams(dimension_semantics=semantics,
                                vmem_limit_bytes=VMEM_LIMIT_BYTES)


def _embed_kernel(xp_ref, xs_ref, g_ref, x_ref, xg_ref, ssq_ref, *, n_prompt_blocks, chunk):
    i = pl.program_id(0)
    rows = x_ref.shape[0]
    g = g_ref[...]

    def copy_from(src_ref):
        def body(c, carry):
            rs = pl.ds(pl.multiple_of(c * chunk, chunk), chunk)
            x = src_ref[rs, :]
            x_ref[rs, :] = x
            xg_ref[rs, :] = (x * g).astype(xg_ref.dtype)
            ssq_ref[rs, :] = jnp.sum(x * x, axis=-1, keepdims=True)
            return carry
        lax.fori_loop(0, rows // chunk, body, 0)

    pl.when(i < n_prompt_blocks)(lambda: copy_from(xp_ref))
    pl.when(i >= n_prompt_blocks)(lambda: copy_from(xs_ref))


def _embed(x_prompt, x_sample, g, *, chunk=16):
    n_prompt, d = x_prompt.shape
    block_rows = x_sample.shape[0]
    assert n_prompt % block_rows == 0 and block_rows % chunk == 0
    n_prompt_blocks = n_prompt // block_rows
    n = n_prompt + block_rows
    row_block = lambda i: (i, 0)
    return pl.pallas_call(
        functools.partial(_embed_kernel, n_prompt_blocks=n_prompt_blocks, chunk=chunk),
        out_shape=(jax.ShapeDtypeStruct((n, d), jnp.float32),
                   jax.ShapeDtypeStruct((n, d), jnp.bfloat16),
                   jax.ShapeDtypeStruct((n, 1), jnp.float32)),
        grid=(n_prompt_blocks + 1,),
        in_specs=[pl.BlockSpec((block_rows, d), lambda i: (jnp.minimum(i, n_prompt_blocks - 1), 0)),
                  pl.BlockSpec((block_rows, d), lambda i: (0, 0)),
                  pl.BlockSpec((1, d), lambda i: (0, 0))],
        out_specs=(pl.BlockSpec((block_rows, d), row_block),
                   pl.BlockSpec((block_rows, d), row_block),
                   pl.BlockSpec((block_rows, 1), row_block)),
        compiler_params=_compiler_params(("arbitrary",)),
        name="embed",
    )(x_prompt, x_sample, g.reshape(1, d))


def _final_norm_kernel(x_ref, ssq_ref, g_ref, yp_ref, ys_ref, *, n_prompt_blocks, chunk):
    i = pl.program_id(0)
    rows, d = x_ref.shape
    g = g_ref[...]

    def write_to(dst_ref):
        def body(c, carry):
            rs = pl.ds(pl.multiple_of(c * chunk, chunk), chunk)
            r = lax.rsqrt(ssq_ref[rs, :] * (1.0 / d) + EPS)
            dst_ref[rs, :] = (x_ref[rs, :] * r * g).astype(dst_ref.dtype)
            return carry
        lax.fori_loop(0, rows // chunk, body, 0)

    pl.when(i < n_prompt_blocks)(lambda: write_to(yp_ref))
    pl.when(i >= n_prompt_blocks)(lambda: write_to(ys_ref))


def _final_norm(x, ssq, g, *, n_prompt, out_dtype, chunk=16):
    n, d = x.shape
    block_rows = n - n_prompt
    assert n_prompt % block_rows == 0 and block_rows % chunk == 0
    n_prompt_blocks = n_prompt // block_rows
    row_block = lambda i: (i, 0)
    return pl.pallas_call(
        functools.partial(_final_norm_kernel, n_prompt_blocks=n_prompt_blocks, chunk=chunk),
        out_shape=(jax.ShapeDtypeStruct((n_prompt, d), out_dtype),
                   jax.ShapeDtypeStruct((block_rows, d), out_dtype)),
        grid=(n_prompt_blocks + 1,),
        in_specs=[pl.BlockSpec((block_rows, d), row_block),
                  pl.BlockSpec((block_rows, 1), row_block),
                  pl.BlockSpec((1, d), lambda i: (0, 0))],
        out_specs=(pl.BlockSpec((block_rows, d), lambda i: (jnp.minimum(i, n_prompt_blocks - 1), 0)),
                   pl.BlockSpec((block_rows, d), lambda i: (0, 0))),
        compiler_params=_compiler_params(("arbitrary",)),
        name="final_norm",
    )(x, ssq, g.reshape(1, d))


def _matmul_kernel(*refs, nk, norm_dim, has_bias, act, has_res, emit_xg, emit_ssq, sub_rows):
    it = iter(refs)
    a_ref = next(it)
    b_ref = next(it)
    ssq_in_ref = next(it) if norm_dim else None
    bias_ref = next(it) if has_bias else None
    res_ref = next(it) if has_res else None
    gain_ref = next(it) if emit_xg else None
    o_ref = next(it)
    xg_ref = next(it) if emit_xg else None
    ssq_out_ref = next(it) if emit_ssq else None
    acc_ref = next(it) if nk > 1 else None

    j = pl.program_id(1)
    k = pl.program_id(2)
    tm = a_ref.shape[0]
    row_slices = [slice(r * sub_rows, (r + 1) * sub_rows) for r in range(tm // sub_rows)]

    if emit_ssq:
        @pl.when((j == 0) & (k == 0))
        def _():
            ssq_out_ref[...] = jnp.zeros_like(ssq_out_ref)

    def epilogue(acc, rs):
        if norm_dim:
            acc = acc * lax.rsqrt(ssq_in_ref[rs, :] * (1.0 / norm_dim) + EPS)
        if has_bias:
            acc = acc + bias_ref[...]
        if act == "relu2":
            r = jnp.maximum(acc, 0.0)
            acc = r * r
        if has_res:
            acc = acc + res_ref[rs, :]
        o_ref[rs, :] = acc.astype(o_ref.dtype)
        if emit_xg:
            xg_ref[rs, :] = (acc * gain_ref[...]).astype(xg_ref.dtype)
        if emit_ssq:
            ssq_out_ref[rs, :] += jnp.sum(acc * acc, axis=-1, keepdims=True)

    def run(first, last):
        for rs in row_slices:
            prod = jnp.dot(a_ref[rs, :], b_ref[...], preferred_element_type=jnp.float32)
            if not first:
                prod = prod + acc_ref[rs, :]
            if last:
                epilogue(prod, rs)
            else:
                acc_ref[rs, :] = prod

    if nk == 1:
        run(True, True)
    else:
        pl.when(k == 0)(lambda: run(True, False))
        pl.when((k > 0) & (k < nk - 1))(lambda: run(False, False))
        pl.when(k == nk - 1)(lambda: run(False, True))


def _matmul(a, b, layer, *, tm, tn, tk=None, ssq=None, bias=None, act=None, res=None, gain=None,
            emit_ssq=False, out_dtype, name, sub_rows=256):
    m, kdim = a.shape
    _, _, n = b.shape
    tk = kdim if tk is None else tk
    assert m % tm == 0 and n % tn == 0 and kdim % tk == 0 and tm % sub_rows == 0
    nk = kdim // tk
    assert nk == 1 or nk >= 3
    in_specs = [pl.BlockSpec((tm, tk), lambda i, j, k: (i, k)),
                pl.BlockSpec((None, tk, tn), lambda i, j, k: (layer, k, j))]
    operands = [a, b]
    if ssq is not None:
        in_specs.append(pl.BlockSpec((tm, 1), lambda i, j, k: (i, 0)))
        operands.append(ssq)
    if bias is not None:
        in_specs.append(pl.BlockSpec((1, tn), lambda i, j, k: (0, j)))
        operands.append(bias.reshape(1, n))
    if res is not None:
        in_specs.append(pl.BlockSpec((tm, tn), lambda i, j, k: (i, j)))
        operands.append(res)
    out_shape = [jax.ShapeDtypeStruct((m, n), out_dtype)]
    out_specs = [pl.BlockSpec((tm, tn), lambda i, j, k: (i, j))]
    if gain is not None:
        in_specs.append(pl.BlockSpec((1, tn), lambda i, j, k: (0, j)))
        operands.append(gain.reshape(1, n))
        out_shape.append(jax.ShapeDtypeStruct((m, n), jnp.bfloat16))
        out_specs.append(pl.BlockSpec((tm, tn), lambda i, j, k: (i, j)))
    if emit_ssq:
        out_shape.append(jax.ShapeDtypeStruct((m, 1), jnp.float32))
        out_specs.append(pl.BlockSpec((tm, 1), lambda i, j, k: (i, 0)))
    scratch = [pltpu.VMEM((tm, tn), jnp.float32)] if nk > 1 else []
    out = pl.pallas_call(
        functools.partial(_matmul_kernel, nk=nk, norm_dim=kdim if ssq is not None else 0,
                          has_bias=bias is not None, act=act, has_res=res is not None,
                          emit_xg=gain is not None, emit_ssq=emit_ssq, sub_rows=sub_rows),
        out_shape=tuple(out_shape),
        grid=(m // tm, n // tn, nk),
        in_specs=in_specs,
        out_specs=tuple(out_specs),
        scratch_shapes=scratch,
        compiler_params=_compiler_params(("arbitrary", "arbitrary", "arbitrary")),
        name=name,
    )(*operands)
    return out if len(out) > 1 else out[0]


def _mixer_kernel(proj_ref, sa_ref, sb_ref, sc_ref, adw_ref, adwb_ref, lng_ref, lnb_ref,
                  bdw_ref, cbd_ref, cs_ref,
                  mix_ref, nap_ref, nbp_ref, ncp_ref, nas_ref, nbs_ref, ncs_ref,
                  ext_a, ext_b, ext_c, conv_a, pool_c,
                  *, n_prompt_steps, sample_rows, a_width, b_width, c_width):
    i = pl.program_id(0)
    block_rows = proj_ref.shape[0]
    row_chunk = MIXER_ROW_CHUNK
    off_gate = a_width
    off_bb = 2 * a_width
    off_bc = off_bb + b_width
    off_bh = off_bc + b_width
    off_cu = off_bh + b_width
    pool_group = c_width // len(POOL_WINDOWS)
    lane = lax.broadcasted_iota(jnp.int32, (row_chunk, V7X_LANES), 1)
    row = lax.broadcasted_iota(jnp.int32, (row_chunk, 1), 0)

    def aligned(start):
        return pl.multiple_of(start, row_chunk)

    def segment(row0, rows, pos0):
        n_chunks = rows // row_chunk

        def block_rows_at(c):
            return pl.ds(aligned(row0 + c * row_chunk), row_chunk)

        def ext_rows_at(c, pad):
            return pl.ds(aligned(c * row_chunk) + pad, row_chunk)

        def window(ext, c, pad, ls=slice(None)):
            return ext[pl.ds(aligned(c * row_chunk), row_chunk + pad), ls]

        def shifted(win, offset):
            sub = offset % V7X_SUBLANES
            base = offset - sub
            if sub:
                win = pltpu.roll(win, win.shape[0] - sub, axis=0)
            return win[base:base + row_chunk]

        def stage_inputs(c, carry_):
            rs = block_rows_at(c)
            a_val = proj_ref[rs, 0:a_width]
            a_gate = proj_ref[rs, off_gate:off_gate + a_width]
            ext_a[ext_rows_at(c, EXT_A_PAD), :] = a_val * jax.nn.sigmoid(a_gate)
            b_c = proj_ref[rs, off_bc:off_bc + b_width]
            b_h = proj_ref[rs, off_bh:off_bh + b_width]
            ext_b[ext_rows_at(c, EXT_B_PAD), :] = b_c * b_h
            ext_c[ext_rows_at(c, EXT_C_PAD), :] = proj_ref[rs, off_cu:off_cu + c_width]
            return carry_

        lax.fori_loop(0, n_chunks, stage_inputs, 0)

        conv_chunk = min(rows, CONV_A_ROW_CHUNK)

        def conv_a_rows(c, carry_):
            for lc in range(a_width // V7X_LANES):
                ls = slice(lc * V7X_LANES, (lc + 1) * V7X_LANES)
                win = ext_a[pl.ds(aligned(c * conv_chunk), conv_chunk + EXT_A_PAD), ls]
                acc = jnp.broadcast_to(adwb_ref[:, ls], (conv_chunk, V7X_LANES))
                for sub in range(V7X_SUBLANES):
                    rot = pltpu.roll(win, win.shape[0] - sub, axis=0) if sub else win
                    for t in range(CONV_A_TAPS):
                        offset = EXT_A_PAD - HIST_A + t
                        if offset % V7X_SUBLANES == sub:
                            base = offset - sub
                            acc = acc + rot[base:base + conv_chunk] * adw_ref[t:t + 1, ls]
                conv_a[pl.ds(aligned(row0 + c * conv_chunk), conv_chunk), ls] = acc
            return carry_

        lax.fori_loop(0, rows // conv_chunk, conv_a_rows, 0)

        def finish_rows(c, carry_):
            rs = block_rows_at(c)
            x = conv_a[rs, :]
            mu = jnp.mean(x, axis=-1, keepdims=True)
            xc = x - mu
            var = jnp.mean(xc * xc, axis=-1, keepdims=True)
            y = xc * lax.rsqrt(var + EPS) * lng_ref[...] + lnb_ref[...]
            mix_ref[rs, 0:a_width] = (y * jax.nn.sigmoid(y)).astype(mix_ref.dtype)

            win_b = window(ext_b, c, EXT_B_PAD)
            conv_b = shifted(win_b, EXT_B_PAD - HIST_B) * bdw_ref[0:1, :]
            for t in range(1, CONV_B_TAPS):
                conv_b = conv_b + shifted(win_b, EXT_B_PAD - HIST_B + t) * bdw_ref[t:t + 1, :]
            b_b = proj_ref[rs, off_bb:off_bb + b_width]
            mix_ref[rs, a_width:a_width + b_width] = (b_b * conv_b).astype(mix_ref.dtype)

            win_c = window(ext_c, c, EXT_C_PAD)
            cur = win_c[EXT_C_PAD:]
            pos = pos0 + c * row_chunk + row
            wsum, col0, span = win_c, 0, 1
            means = []
            for g, w in enumerate(POOL_WINDOWS):
                start = (g * pool_group) // V7X_LANES * V7X_LANES
                wsum, col0 = wsum[:, start - col0:], start
                while span < w:
                    wsum = wsum + pltpu.roll(wsum, span, axis=0)
                    span *= 2
                inv_cnt = 1.0 / jnp.minimum(pos + 1, w).astype(jnp.float32)
                means.append((wsum[EXT_C_PAD:] * inv_cnt, col0))
            tiles = []
            for lo in range(0, c_width, V7X_LANES):
                g_lo = lo // pool_group
                g_hi = (lo + V7X_LANES - 1) // pool_group
                tile_of = lambda g: means[g][0][:, lo - means[g][1]:lo - means[g][1] + V7X_LANES]
                if g_lo == g_hi:
                    tiles.append(tile_of(g_lo))
                else:
                    tiles.append(jnp.where(lane < g_hi * pool_group - lo, tile_of(g_lo),
                                           tile_of(g_hi)))
            pooled = jnp.concatenate(tiles, axis=-1)
            pool_c[rs, :] = (pooled - cur).astype(pool_c.dtype)
            return carry_

        lax.fori_loop(0, n_chunks, finish_rows, 0, unroll=2 if n_chunks % 2 == 0 else 1)

    def load_history(hist_a, hist_b, hist_c):
        ext_a[0:EXT_A_PAD, :] = jnp.zeros((EXT_A_PAD, a_width), jnp.float32)
        ext_b[0:EXT_B_PAD, :] = jnp.zeros((EXT_B_PAD, b_width), jnp.float32)
        ext_c[0:EXT_C_PAD, :] = jnp.zeros((EXT_C_PAD, c_width), jnp.float32)
        if hist_a is not None:
            ext_a[EXT_A_PAD - HIST_A:EXT_A_PAD, :] = hist_a
            ext_b[EXT_B_PAD - HIST_B:EXT_B_PAD, :] = hist_b
            ext_c[EXT_C_PAD - HIST_C:EXT_C_PAD, :] = hist_c

    def new_state(rows):
        return (ext_a[rows + EXT_A_PAD - HIST_A:rows + EXT_A_PAD, :],
                ext_b[rows + EXT_B_PAD - HIST_B:rows + EXT_B_PAD, :],
                ext_c[rows + EXT_C_PAD - HIST_C:rows + EXT_C_PAD, :])

    @pl.when(i < n_prompt_steps)
    def _prompt_step():
        pl.when(i == 0)(lambda: load_history(None, None, None))
        segment(0, block_rows, i * block_rows)
        nap_ref[0], nbp_ref[0], ncp_ref[0] = new_state(block_rows)
        ext_a[0:EXT_A_PAD, :] = ext_a[block_rows:block_rows + EXT_A_PAD, :]
        ext_b[0:EXT_B_PAD, :] = ext_b[block_rows:block_rows + EXT_B_PAD, :]
        ext_c[0:EXT_C_PAD, :] = ext_c[block_rows:block_rows + EXT_C_PAD, :]

    @pl.when(i >= n_prompt_steps)
    def _sample_step():
        def one_stream(s, carry_):
            load_history(sa_ref[s], sb_ref[s], sc_ref[s])
            segment(s * sample_rows, sample_rows, PAST_LEN)
            nas_ref[s], nbs_ref[s], ncs_ref[s] = new_state(sample_rows)
            return carry_
        lax.fori_loop(0, block_rows // sample_rows, one_stream, 0)

    c_lin = jnp.dot(pool_c[...], cbd_ref[...], preferred_element_type=jnp.float32)
    mix_ref[:, a_width + b_width:] = (c_lin * cs_ref[...]).astype(mix_ref.dtype)


def _mixer(proj, state_a, state_b, state_c, a_dw, a_dw_b, a_ln_g, a_ln_b, b_dw, c_bd, c_scale,
           *, n_prompt, sample_rows, block_rows=256):
    n_rows, in_cols = proj.shape
    a_width = a_dw.shape[1]
    b_width = b_dw.shape[1]
    c_width = c_scale.shape[0]
    mix_width = a_width + b_width + c_width
    n_streams = state_a.shape[0]
    streams_per_step = block_rows // sample_rows
    assert n_prompt % block_rows == 0 and block_rows % sample_rows == 0
    assert sample_rows % MIXER_ROW_CHUNK == 0 and sample_rows >= HIST_A
    assert n_rows == n_prompt + n_streams * sample_rows and n_streams % streams_per_step == 0
    n_prompt_steps = n_prompt // block_rows
    n_steps = n_prompt_steps + n_streams // streams_per_step

    def const(shape):
        return pl.BlockSpec(shape, lambda s: tuple(0 for _ in shape))

    def sample_state(h, w):
        return pl.BlockSpec((streams_per_step, h, w),
                            lambda s: (jnp.maximum(s - n_prompt_steps, 0), 0, 0))

    f32 = jnp.float32
    kern = functools.partial(_mixer_kernel, n_prompt_steps=n_prompt_steps,
                             sample_rows=sample_rows, a_width=a_width, b_width=b_width,
                             c_width=c_width)
    return pl.pallas_call(
        kern,
        out_shape=(jax.ShapeDtypeStruct((n_rows, mix_width), jnp.bfloat16),
                   jax.ShapeDtypeStruct((1, HIST_A, a_width), f32),
                   jax.ShapeDtypeStruct((1, HIST_B, b_width), f32),
                   jax.ShapeDtypeStruct((1, HIST_C, c_width), f32),
                   jax.ShapeDtypeStruct((n_streams, HIST_A, a_width), f32),
                   jax.ShapeDtypeStruct((n_streams, HIST_B, b_width), f32),
                   jax.ShapeDtypeStruct((n_streams, HIST_C, c_width), f32)),
        grid=(n_steps,),
        in_specs=[pl.BlockSpec((block_rows, in_cols), lambda s: (s, 0)),
                  sample_state(HIST_A, a_width),
                  sample_state(HIST_B, b_width),
                  sample_state(HIST_C, c_width),
                  const((CONV_A_TAPS, a_width)),
                  const((1, a_width)), const((1, a_width)), const((1, a_width)),
                  const((CONV_B_TAPS, b_width)),
                  const((c_width, c_width)),
                  const((1, c_width))],
        out_specs=(pl.BlockSpec((block_rows, mix_width), lambda s: (s, 0)),
                   const((1, HIST_A, a_width)),
                   const((1, HIST_B, b_width)),
                   const((1, HIST_C, c_width)),
                   sample_state(HIST_A, a_width),
                   sample_state(HIST_B, b_width),
                   sample_state(HIST_C, c_width)),
        scratch_shapes=[pltpu.VMEM((block_rows + EXT_A_PAD, a_width), f32),
                        pltpu.VMEM((block_rows + EXT_B_PAD, b_width), f32),
                        pltpu.VMEM((block_rows + EXT_C_PAD, c_width), f32),
                        pltpu.VMEM((block_rows, a_width), f32),
                        pltpu.VMEM((block_rows, c_width), jnp.bfloat16)],
        compiler_params=_compiler_params(("arbitrary",)),
        name="mixer",
    )(proj, state_a, state_b, state_c, a_dw, a_dw_b.reshape(1, -1), a_ln_g.reshape(1, -1),
      a_ln_b.reshape(1, -1), b_dw, c_bd, c_scale.reshape(1, -1))


def _block_diag(c_w):
    g, n, _ = c_w.shape
    eye = jnp.eye(g, dtype=c_w.dtype)
    return (eye[:, None, :, None] * c_w[:, :, None, :]).reshape(g * n, g * n)


def kernel(x_prompt, x_sample, state_conv_a, state_conv_b, state_pool, norm_mix, w_in, b_in,
           a_dw, a_dw_b, a_ln_g, a_ln_b, b_dw, c_w, c_scale, w_out, norm_ffn, w_up, w_down,
           norm_final):
    depth = w_in.shape[0]
    bp, tp, d_model = x_prompt.shape
    bs, ts, _ = x_sample.shape
    assert bp == 1, "prompt rows must form one stream"
    n_prompt = bp * tp
    n_sample = bs * ts
    bf16 = jnp.bfloat16
    f32 = jnp.float32

    w_in_b = w_in.astype(bf16)
    w_out_b = w_out.astype(bf16)
    w_up_b = w_up.astype(bf16)
    w_down_b = w_down.astype(bf16)

    tm = 1536
    assert (n_prompt + n_sample) % tm == 0

    x, xg, ssq = _embed(x_prompt.reshape(n_prompt, d_model).astype(f32),
                        x_sample.reshape(n_sample, d_model).astype(f32), norm_mix[0])
    new_p = ([], [], [])
    new_s = ([], [], [])
    for l in range(depth):
        proj = _matmul(xg, w_in_b, l, tm=tm, tn=1024, ssq=ssq, bias=b_in[l], out_dtype=f32,
                       name="in_proj")
        c_bd = _block_diag(c_w[l]).astype(bf16)
        mix, na_p, nb_p, nc_p, na_s, nb_s, nc_s = _mixer(
            proj, state_conv_a[l].astype(f32), state_conv_b[l].astype(f32),
            state_pool[l].astype(f32), a_dw[l], a_dw_b[l], a_ln_g[l], a_ln_b[l], b_dw[l], c_bd,
            c_scale[l], n_prompt=n_prompt, sample_rows=ts)
        x, hg, ssq = _matmul(mix, w_out_b, l, tm=tm, tn=512, res=x, gain=norm_ffn[l],
                             emit_ssq=True, out_dtype=f32, name="out_proj")
        u = _matmul(hg, w_up_b, l, tm=tm, tn=1024, ssq=ssq, act="relu2", out_dtype=bf16,
                    name="ffn_up")
        if l + 1 < depth:
            x, xg, ssq = _matmul(u, w_down_b, l, tm=tm, tn=1024, tk=2048, res=x,
                                 gain=norm_mix[l + 1], emit_ssq=True, out_dtype=f32,
                                 name="ffn_down")
        else:
            x, ssq = _matmul(u, w_down_b, l, tm=tm, tn=1024, tk=2048, res=x, emit_ssq=True,
                             out_dtype=f32, name="ffn_down_last")
        for dst, val in zip(new_p + new_s, (na_p, nb_p, nc_p, na_s, nb_s, nc_s)):
            dst.append(val)

    y_prompt, y_sample = _final_norm(x, ssq, norm_final, n_prompt=n_prompt,
                                     out_dtype=x_prompt.dtype)
    stack = lambda parts: jnp.stack(parts, 0)
    return (y_prompt.reshape(bp, tp, d_model), y_sample.reshape(bs, ts, d_model),
            stack(new_p[0]), stack(new_p[1]), stack(new_p[2]),
            stack(new_s[0]), stack(new_s[1]), stack(new_s[2]))
```

```python
import functools

import jax
import jax.numpy as jnp
from jax import lax
from jax.experimental import pallas as pl
from jax.experimental.pallas import tpu as pltpu

EPS = 1e-6
PAST_LEN = 4096
POOL_WINDOWS = (2, 4, 8, 16)
CONV_A_TAPS = 31
CONV_B_TAPS = 3
HIST_A = CONV_A_TAPS - 1
HIST_B = CONV_B_TAPS - 1
HIST_C = max(POOL_WINDOWS) - 1

V7X_VMEM_BYTES = 64 * 1024 * 1024
V7X_SUBLANES = 8
V7X_LANES = 128
VMEM_LIMIT_BYTES = V7X_VMEM_BYTES - 4 * 1024 * 1024

EXT_A_PAD = 32
EXT_B_PAD = 8
EXT_C_PAD = 16
MIXER_ROW_CHUNK = 32
CONV_A_ROW_CHUNK = 64

assert all(w & (w - 1) == 0 for w in POOL_WINDOWS) and list(POOL_WINDOWS) == sorted(POOL_WINDOWS)


def _compiler_params(semantics):
    return pltpu.CompilerParams(dimension_semantics=semantics,
                                vmem_limit_bytes=VMEM_LIMIT_BYTES)


def _embed_kernel(xp_ref, xs_ref, g_ref, x_ref, xg_ref, ssq_ref, *, n_prompt_blocks, chunk):
    i = pl.program_id(0)
    rows = x_ref.shape[0]
    g = g_ref[...]

    def copy_from(src_ref):
        def body(c, carry):
            rs = pl.ds(pl.multiple_of(c * chunk, chunk), chunk)
            x = src_ref[rs, :]
            x_ref[rs, :] = x
            xg_ref[rs, :] = (x * g).astype(xg_ref.dtype)
            ssq_ref[rs, :] = jnp.sum(x * x, axis=-1, keepdims=True)
            return carry
        lax.fori_loop(0, rows // chunk, body, 0)

    pl.when(i < n_prompt_blocks)(lambda: copy_from(xp_ref))
    pl.when(i >= n_prompt_blocks)(lambda: copy_from(xs_ref))


def _embed(x_prompt, x_sample, g, *, chunk=16):
    n_prompt, d = x_prompt.shape
    block_rows = x_sample.shape[0]
    assert n_prompt % block_rows == 0 and block_rows % chunk == 0
    n_prompt_blocks = n_prompt // block_rows
    n = n_prompt + block_rows
    row_block = lambda i: (i, 0)
    return pl.pallas_call(
        functools.partial(_embed_kernel, n_prompt_blocks=n_prompt_blocks, chunk=chunk),
        out_shape=(jax.ShapeDtypeStruct((n, d), jnp.float32),
                   jax.ShapeDtypeStruct((n, d), jnp.bfloat16),
                   jax.ShapeDtypeStruct((n, 1), jnp.float32)),
        grid=(n_prompt_blocks + 1,),
        in_specs=[pl.BlockSpec((block_rows, d), lambda i: (jnp.minimum(i, n_prompt_blocks - 1), 0)),
                  pl.BlockSpec((block_rows, d), lambda i: (0, 0)),
                  pl.BlockSpec((1, d), lambda i: (0, 0))],
        out_specs=(pl.BlockSpec((block_rows, d), row_block),
                   pl.BlockSpec((block_rows, d), row_block),
                   pl.BlockSpec((block_rows, 1), row_block)),
        compiler_params=_compiler_params(("arbitrary",)),
        name="embed",
    )(x_prompt, x_sample, g.reshape(1, d))


def _final_norm_kernel(x_ref, ssq_ref, g_ref, yp_ref, ys_ref, *, n_prompt_blocks, chunk):
    i = pl.program_id(0)
    rows, d = x_ref.shape
    g = g_ref[...]

    def write_to(dst_ref):
        def body(c, carry):
            rs = pl.ds(pl.multiple_of(c * chunk, chunk), chunk)
            r = lax.rsqrt(ssq_ref[rs, :] * (1.0 / d) + EPS)
            dst_ref[rs, :] = (x_ref[rs, :] * r * g).astype(dst_ref.dtype)
            return carry
        lax.fori_loop(0, rows // chunk, body, 0)

    pl.when(i < n_prompt_blocks)(lambda: write_to(yp_ref))
    pl.when(i >= n_prompt_blocks)(lambda: write_to(ys_ref))


def _final_norm(x, ssq, g, *, n_prompt, out_dtype, chunk=16):
    n, d = x.shape
    block_rows = n - n_prompt
    assert n_prompt % block_rows == 0 and block_rows % chunk == 0
    n_prompt_blocks = n_prompt // block_rows
    row_block = lambda i: (i, 0)
    return pl.pallas_call(
        functools.partial(_final_norm_kernel, n_prompt_blocks=n_prompt_blocks, chunk=chunk),
        out_shape=(jax.ShapeDtypeStruct((n_prompt, d), out_dtype),
                   jax.ShapeDtypeStruct((block_rows, d), out_dtype)),
        grid=(n_prompt_blocks + 1,),
        in_specs=[pl.BlockSpec((block_rows, d), row_block),
                  pl.BlockSpec((block_rows, 1), row_block),
                  pl.BlockSpec((1, d), lambda i: (0, 0))],
        out_specs=(pl.BlockSpec((block_rows, d), lambda i: (jnp.minimum(i, n_prompt_blocks - 1), 0)),
                   pl.BlockSpec((block_rows, d), lambda i: (0, 0))),
        compiler_params=_compiler_params(("arbitrary",)),
        name="final_norm",
    )(x, ssq, g.reshape(1, d))


def _matmul_kernel(*refs, nk, norm_dim, has_bias, act, has_res, emit_xg, emit_ssq, sub_rows):
    it = iter(refs)
    a_ref = next(it)
    b_ref = next(it)
    ssq_in_ref = next(it) if norm_dim else None
    bias_ref = next(it) if has_bias else None
    res_ref = next(it) if has_res else None
    gain_ref = next(it) if emit_xg else None
    o_ref = next(it)
    xg_ref = next(it) if emit_xg else None
    ssq_out_ref = next(it) if emit_ssq else None
    acc_ref = next(it) if nk > 1 else None

    j = pl.program_id(1)
    k = pl.program_id(2)
    tm = a_ref.shape[0]
    row_slices = [slice(r * sub_rows, (r + 1) * sub_rows) for r in range(tm // sub_rows)]

    if emit_ssq:
        @pl.when((j == 0) & (k == 0))
        def _():
            ssq_out_ref[...] = jnp.zeros_like(ssq_out_ref)

    def epilogue(acc, rs):
        if norm_dim:
            acc = acc * lax.rsqrt(ssq_in_ref[rs, :] * (1.0 / norm_dim) + EPS)
        if has_bias:
            acc = acc + bias_ref[...]
        if act == "relu2":
            r = jnp.maximum(acc, 0.0)
            acc = r * r
        if has_res:
            acc = acc + res_ref[rs, :]
        o_ref[rs, :] = acc.astype(o_ref.dtype)
        if emit_xg:
            xg_ref[rs, :] = (acc * gain_ref[...]).astype(xg_ref.dtype)
        if emit_ssq:
            ssq_out_ref[rs, :] += jnp.sum(acc * acc, axis=-1, keepdims=True)

    def run(first, last):
        for rs in row_slices:
            prod = jnp.dot(a_ref[rs, :], b_ref[...], preferred_element_type=jnp.float32)
            if not first:
                prod = prod + acc_ref[rs, :]
            if last:
                epilogue(prod, rs)
            else:
                acc_ref[rs, :] = prod

    if nk == 1:
        run(True, True)
    else:
        pl.when(k == 0)(lambda: run(True, False))
        pl.when((k > 0) & (k < nk - 1))(lambda: run(False, False))
        pl.when(k == nk - 1)(lambda: run(False, True))


def _matmul(a, b, layer, *, tm, tn, tk=None, ssq=None, bias=None, act=None, res=None, gain=None,
            emit_ssq=False, out_dtype, name, sub_rows=256):
    m, kdim = a.shape
    _, _, n = b.shape
    tk = kdim if tk is None else tk
    assert m % tm == 0 and n % tn == 0 and kdim % tk == 0 and tm % sub_rows == 0
    nk = kdim // tk
    assert nk == 1 or nk >= 3
    in_specs = [pl.BlockSpec((tm, tk), lambda i, j, k: (i, k)),
                pl.BlockSpec((None, tk, tn), lambda i, j, k: (layer, k, j))]
    operands = [a, b]
    if ssq is not None:
        in_specs.append(pl.BlockSpec((tm, 1), lambda i, j, k: (i, 0)))
        operands.append(ssq)
    if bias is not None:
        in_specs.append(pl.BlockSpec((1, tn), lambda i, j, k: (0, j)))
        operands.append(bias.reshape(1, n))
    if res is not None:
        in_specs.append(pl.BlockSpec((tm, tn), lambda i, j, k: (i, j)))
        operands.append(res)
    out_shape = [jax.ShapeDtypeStruct((m, n), out_dtype)]
    out_specs = [pl.BlockSpec((tm, tn), lambda i, j, k: (i, j))]
    if gain is not None:
        in_specs.append(pl.BlockSpec((1, tn), lambda i, j, k: (0, j)))
        operands.append(gain.reshape(1, n))
        out_shape.append(jax.ShapeDtypeStruct((m, n), jnp.bfloat16))
        out_specs.append(pl.BlockSpec((tm, tn), lambda i, j, k: (i, j)))
    if emit_ssq:
        out_shape.append(jax.ShapeDtypeStruct((m, 1), jnp.float32))
        out_specs.append(pl.BlockSpec((tm, 1), lambda i, j, k: (i, 0)))
    scratch = [pltpu.VMEM((tm, tn), jnp.float32)] if nk > 1 else []
    out = pl.pallas_call(
        functools.partial(_matmul_kernel, nk=nk, norm_dim=kdim if ssq is not None else 0,
                          has_bias=bias is not None, act=act, has_res=res is not None,
                          emit_xg=gain is not None, emit_ssq=emit_ssq, sub_rows=sub_rows),
        out_shape=tuple(out_shape),
        grid=(m // tm, n // tn, nk),
        in_specs=in_specs,
        out_specs=tuple(out_specs),
        scratch_shapes=scratch,
        compiler_params=_compiler_params(("arbitrary", "arbitrary", "arbitrary")),
        name=name,
    )(*operands)
    return out if len(out) > 1 else out[0]


def _mixer_rows(proj_ref, prm, scr, mix_scr, mix_row0, row0, rows, pos0, widths):
    adw_ref, adwb_ref, lng_ref, lnb_ref, bdw_ref = prm
    ext_a, ext_b, ext_c, conv_a, pool_c = scr
    a_width, b_width, c_width = widths
    rc = MIXER_ROW_CHUNK
    off_gate = a_width
    off_bb = 2 * a_width
    off_bc = off_bb + b_width
    off_bh = off_bc + b_width
    off_cu = off_bh + b_width
    pool_group = c_width // len(POOL_WINDOWS)
    lane = lax.broadcasted_iota(jnp.int32, (rc, V7X_LANES), 1)
    row = lax.broadcasted_iota(jnp.int32, (rc, 1), 0)

    def shifted(win, offset):
        sub = offset % V7X_SUBLANES
        base = offset - sub
        if sub:
            win = pltpu.roll(win, win.shape[0] - sub, axis=0)
        return win[base:base + rc]

    for c in range(rows // rc):
        rs = slice(row0 + c * rc, row0 + (c + 1) * rc)
        a_val = proj_ref[rs, 0:a_width]
        a_gate = proj_ref[rs, off_gate:off_gate + a_width]
        ext_a[EXT_A_PAD + c * rc:EXT_A_PAD + (c + 1) * rc, :] = a_val * jax.nn.sigmoid(a_gate)
        b_c = proj_ref[rs, off_bc:off_bc + b_width]
        b_h = proj_ref[rs, off_bh:off_bh + b_width]
        ext_b[EXT_B_PAD + c * rc:EXT_B_PAD + (c + 1) * rc, :] = b_c * b_h
        ext_c[EXT_C_PAD + c * rc:EXT_C_PAD + (c + 1) * rc, :] = proj_ref[rs, off_cu:off_cu + c_width]
        yield

    cc = min(rows, CONV_A_ROW_CHUNK)
    for c in range(rows // cc):
        for lc in range(a_width // V7X_LANES):
            ls = slice(lc * V7X_LANES, (lc + 1) * V7X_LANES)
            win = ext_a[c * cc:(c + 1) * cc + EXT_A_PAD, ls]
            acc = jnp.broadcast_to(adwb_ref[:, ls], (cc, V7X_LANES))
            for sub in range(V7X_SUBLANES):
                rot = pltpu.roll(win, win.shape[0] - sub, axis=0) if sub else win
                for t in range(CONV_A_TAPS):
                    offset = EXT_A_PAD - HIST_A + t
                    if offset % V7X_SUBLANES == sub:
                        base = offset - sub
                        acc = acc + rot[base:base + cc] * adw_ref[t:t + 1, ls]
            conv_a[row0 + c * cc:row0 + (c + 1) * cc, ls] = acc
            yield

    for c in range(rows // rc):
        rs = slice(row0 + c * rc, row0 + (c + 1) * rc)
        dst = pl.ds(mix_row0 + (row0 + c * rc), rc)
        x = conv_a[rs, :]
        mu = jnp.mean(x, axis=-1, keepdims=True)
        xc = x - mu
        var = jnp.mean(xc * xc, axis=-1, keepdims=True)
        y = xc * lax.rsqrt(var + EPS) * lng_ref[...] + lnb_ref[...]
        mix_scr[dst, 0:a_width] = (y * jax.nn.sigmoid(y)).astype(mix_scr.dtype)

        win_b = ext_b[c * rc:(c + 1) * rc + EXT_B_PAD, :]
        conv_b = shifted(win_b, EXT_B_PAD - HIST_B) * bdw_ref[0:1, :]
        for t in range(1, CONV_B_TAPS):
            conv_b = conv_b + shifted(win_b, EXT_B_PAD - HIST_B + t) * bdw_ref[t:t + 1, :]
        b_b = proj_ref[rs, off_bb:off_bb + b_width]
        mix_scr[dst, a_width:a_width + b_width] = (b_b * conv_b).astype(mix_scr.dtype)

        win_c = ext_c[c * rc:(c + 1) * rc + EXT_C_PAD, :]
        cur = win_c[EXT_C_PAD:]
        pos = pos0 + c * rc + row
        wsum, col0, span = win_c, 0, 1
        means = []
        for g, w in enumerate(POOL_WINDOWS):
            start = (g * pool_group) // V7X_LANES * V7X_LANES
            wsum, col0 = wsum[:, start - col0:], start
            while span < w:
                wsum = wsum + pltpu.roll(wsum, span, axis=0)
                span *= 2
            inv_cnt = 1.0 / jnp.minimum(pos + 1, w).astype(jnp.float32)
            means.append((wsum[EXT_C_PAD:] * inv_cnt, col0))
        tiles = []
        for lo in range(0, c_width, V7X_LANES):
            g_lo = lo // pool_group
            g_hi = (lo + V7X_LANES - 1) // pool_group
            tile_of = lambda g: means[g][0][:, lo - means[g][1]:lo - means[g][1] + V7X_LANES]
            if g_lo == g_hi:
                tiles.append(tile_of(g_lo))
            else:
                tiles.append(jnp.where(lane < g_hi * pool_group - lo, tile_of(g_lo),
                                       tile_of(g_hi)))
        pooled = jnp.concatenate(tiles, axis=-1)
        pool_c[rs, :] = (pooled - cur).astype(pool_c.dtype)
        yield


def _mix_out_kernel(proj_ref, sa_ref, sb_ref, sc_ref, adw_ref, adwb_ref, lng_ref, lnb_ref,
                    bdw_ref, cbd_ref, cs_ref, w_ref, res_ref, gain_ref,
                    o_ref, xg_ref, ssq_ref, nap_ref, nbp_ref, ncp_ref, nas_ref, nbs_ref, ncs_ref,
                    mix_even, mix_odd, ext_a, ext_b, ext_c, conv_a, pool_c,
                    *, n_prompt_tiles, sample_chunks, sample_rows, widths, sub_rows):
    i = pl.program_id(0)
    j = pl.program_id(1)
    a_width, b_width, c_width = widths
    tile_rows = o_ref.shape[0]
    chunk_rows = proj_ref.shape[0]
    prm = (adw_ref, adwb_ref, lng_ref, lnb_ref, bdw_ref)
    scr = (ext_a, ext_b, ext_c, conv_a, pool_c)
    mix_row0 = pl.multiple_of(j * chunk_rows, chunk_rows)

    def set_history(hist_a, hist_b, hist_c):
        ext_a[0:EXT_A_PAD, :] = jnp.zeros((EXT_A_PAD, a_width), jnp.float32)
        ext_b[0:EXT_B_PAD, :] = jnp.zeros((EXT_B_PAD, b_width), jnp.float32)
        ext_c[0:EXT_C_PAD, :] = jnp.zeros((EXT_C_PAD, c_width), jnp.float32)
        if hist_a is not None:
            ext_a[EXT_A_PAD - HIST_A:EXT_A_PAD, :] = hist_a
            ext_b[EXT_B_PAD - HIST_B:EXT_B_PAD, :] = hist_b
            ext_c[EXT_C_PAD - HIST_C:EXT_C_PAD, :] = hist_c

    def new_state(rows):
        return (ext_a[rows + EXT_A_PAD - HIST_A:rows + EXT_A_PAD, :],
                ext_b[rows + EXT_B_PAD - HIST_B:rows + EXT_B_PAD, :],
                ext_c[rows + EXT_C_PAD - HIST_C:rows + EXT_C_PAD, :])

    def pooled_map(mix_scr):
        c_lin = jnp.dot(pool_c[...], cbd_ref[...], preferred_element_type=jnp.float32)
        mix_scr[pl.ds(mix_row0, chunk_rows), a_width + b_width:] = (
            c_lin * cs_ref[...]).astype(mix_scr.dtype)

    def prompt_chunk(mix_scr):
        yield from _mixer_rows(proj_ref, prm, scr, mix_scr, mix_row0, 0, chunk_rows,
                               i * tile_rows + j * chunk_rows, widths)
        pooled_map(mix_scr)
        nap_ref[0], nbp_ref[0], ncp_ref[0] = new_state(chunk_rows)
        ext_a[0:EXT_A_PAD, :] = ext_a[chunk_rows:chunk_rows + EXT_A_PAD, :]
        ext_b[0:EXT_B_PAD, :] = ext_b[chunk_rows:chunk_rows + EXT_B_PAD, :]
        ext_c[0:EXT_C_PAD, :] = ext_c[chunk_rows:chunk_rows + EXT_C_PAD, :]
        yield

    def sample_chunk(mix_scr):
        for s in range(chunk_rows // sample_rows):
            set_history(sa_ref[s], sb_ref[s], sc_ref[s])
            yield from _mixer_rows(proj_ref, prm, scr, mix_scr, mix_row0, s * sample_rows,
                                   sample_rows, PAST_LEN, widths)
            nas_ref[s], nbs_ref[s], ncs_ref[s] = new_state(sample_rows)
        pooled_map(mix_scr)
        yield

    def project(mix_scr):
        for r in range(tile_rows // sub_rows):
            rs = slice(r * sub_rows, (r + 1) * sub_rows)
            acc = jnp.dot(mix_scr[rs, :], w_ref[...], preferred_element_type=jnp.float32)
            acc = acc + res_ref[rs, :]
            o_ref[rs, :] = acc
            xg_ref[rs, :] = (acc * gain_ref[...]).astype(xg_ref.dtype)
            ssq_ref[rs, :] += jnp.sum(acc * acc, axis=-1, keepdims=True)
            yield

    def run(matmul_units, mixer_units, mixers_per_matmul):
        for _ in matmul_units or ():
            for _ in range(mixers_per_matmul):
                if mixer_units is None or next(mixer_units, "done") == "done":
                    break
        for _ in mixer_units or ():
            pass

    def units_per_matmul(rows, segments):
        per_segment = (2 * (rows // MIXER_ROW_CHUNK)
                       + (rows // min(rows, CONV_A_ROW_CHUNK)) * (a_width // V7X_LANES))
        return -(-(segments * per_segment + 1) // (tile_rows // sub_rows))

    prompt_rate = units_per_matmul(chunk_rows, 1)
    sample_rate = units_per_matmul(sample_rows, chunk_rows // sample_rows)

    @pl.when((i >= 1) & (j == 0))
    def _():
        ssq_ref[...] = jnp.zeros_like(ssq_ref)

    is_even = i % 2 == 0
    in_prompt = i < n_prompt_tiles
    sample_tile = n_prompt_tiles

    @pl.when(i == 0)
    def _():
        pl.when(j == 0)(lambda: set_history(None, None, None))
        run(None, prompt_chunk(mix_even), 0)

    @pl.when((i >= 1) & in_prompt & is_even)
    def _():
        run(project(mix_odd), prompt_chunk(mix_even), prompt_rate)

    @pl.when(in_prompt & jnp.logical_not(is_even))
    def _():
        run(project(mix_even), prompt_chunk(mix_odd), prompt_rate)

    @pl.when((i == sample_tile) & (j < sample_chunks))
    def _():
        run(project(mix_odd), sample_chunk(mix_even), sample_rate)

    @pl.when((i == sample_tile) & (j >= sample_chunks))
    def _():
        run(project(mix_odd), None, 0)

    @pl.when(i == sample_tile + 1)
    def _():
        run(project(mix_even), None, 0)


def _mix_out(proj, x, state_a, state_b, state_c, a_dw, a_dw_b, a_ln_g, a_ln_b, b_dw, c_bd,
             c_scale, w_out, layer, gain, *, n_prompt, sample_rows, tile_rows=1024, tn=512,
             sub_rows=512):
    n_rows, in_cols = proj.shape
    d_model = x.shape[1]
    a_width = a_dw.shape[1]
    b_width = b_dw.shape[1]
    c_width = c_scale.shape[0]
    widths = (a_width, b_width, c_width)
    mix_width = a_width + b_width + c_width
    n_streams = state_a.shape[0]
    n_col = d_model // tn
    chunk_rows = tile_rows // n_col
    streams_per_chunk = chunk_rows // sample_rows
    n_prompt_tiles = n_prompt // tile_rows
    n_sample = n_rows - n_prompt
    sample_chunks = n_sample // chunk_rows
    assert d_model % tn == 0 and tile_rows % n_col == 0 and tile_rows % sub_rows == 0
    assert n_prompt % tile_rows == 0 and n_prompt_tiles % 2 == 0
    assert chunk_rows % sample_rows == 0 and chunk_rows % CONV_A_ROW_CHUNK == 0
    assert sample_rows % MIXER_ROW_CHUNK == 0 and sample_rows >= HIST_A
    assert n_sample == n_streams * sample_rows and n_sample % chunk_rows == 0
    assert 0 < sample_chunks <= n_col and mix_width == w_out.shape[1]
    last_chunk = n_rows // chunk_rows - 1
    first_sample_chunk = n_prompt // chunk_rows

    def const(shape):
        return pl.BlockSpec(shape, lambda i, j: tuple(0 for _ in shape))

    def sample_state(h, w):
        def index(i, j):
            c = jnp.clip(i * n_col + j - first_sample_chunk, 0, sample_chunks - 1)
            return (c, 0, 0)
        return pl.BlockSpec((streams_per_chunk, h, w), index)

    def out_block(i, j):
        return (jnp.maximum(i - 1, 0), jnp.where(i == 0, 0, j))

    f32 = jnp.float32
    kern = functools.partial(_mix_out_kernel, n_prompt_tiles=n_prompt_tiles,
                             sample_chunks=sample_chunks, sample_rows=sample_rows, widths=widths,
                             sub_rows=sub_rows)
    return pl.pallas_call(
        kern,
        out_shape=(jax.ShapeDtypeStruct((n_rows, d_model), f32),
                   jax.ShapeDtypeStruct((n_rows, d_model), jnp.bfloat16),
                   jax.ShapeDtypeStruct((n_rows, 1), f32),
                   jax.ShapeDtypeStruct((1, HIST_A, a_width), f32),
                   jax.ShapeDtypeStruct((1, HIST_B, b_width), f32),
                   jax.ShapeDtypeStruct((1, HIST_C, c_width), f32),
                   jax.ShapeDtypeStruct((n_streams, HIST_A, a_width), f32),
                   jax.ShapeDtypeStruct((n_streams, HIST_B, b_width), f32),
                   jax.ShapeDtypeStruct((n_streams, HIST_C, c_width), f32)),
        grid=(n_prompt_tiles + 2, n_col),
        in_specs=[pl.BlockSpec((chunk_rows, in_cols),
                               lambda i, j: (jnp.minimum(i * n_col + j, last_chunk), 0)),
                  sample_state(HIST_A, a_width),
                  sample_state(HIST_B, b_width),
                  sample_state(HIST_C, c_width),
                  const((CONV_A_TAPS, a_width)),
                  const((1, a_width)), const((1, a_width)), const((1, a_width)),
                  const((CONV_B_TAPS, b_width)),
                  const((c_width, c_width)),
                  const((1, c_width)),
                  pl.BlockSpec((None, mix_width, tn), lambda i, j: (layer, 0, j)),
                  pl.BlockSpec((tile_rows, tn), out_block),
                  pl.BlockSpec((1, tn), lambda i, j: (0, j))],
        out_specs=(pl.BlockSpec((tile_rows, tn), out_block),
                   pl.BlockSpec((tile_rows, tn), out_block),
                   pl.BlockSpec((tile_rows, 1), lambda i, j: (jnp.maximum(i - 1, 0), 0)),
                   const((1, HIST_A, a_width)),
                   const((1, HIST_B, b_width)),
                   const((1, HIST_C, c_width)),
                   sample_state(HIST_A, a_width),
                   sample_state(HIST_B, b_width),
                   sample_state(HIST_C, c_width)),
        scratch_shapes=[pltpu.VMEM((tile_rows, mix_width), jnp.bfloat16),
                        pltpu.VMEM((tile_rows, mix_width), jnp.bfloat16),
                        pltpu.VMEM((chunk_rows + EXT_A_PAD, a_width), f32),
                        pltpu.VMEM((chunk_rows + EXT_B_PAD, b_width), f32),
                        pltpu.VMEM((chunk_rows + EXT_C_PAD, c_width), f32),
                        pltpu.VMEM((chunk_rows, a_width), f32),
                        pltpu.VMEM((chunk_rows, c_width), jnp.bfloat16)],
        compiler_params=_compiler_params(("arbitrary", "arbitrary")),
        name="mix_out",
    )(proj, state_a, state_b, state_c, a_dw, a_dw_b.reshape(1, -1), a_ln_g.reshape(1, -1),
      a_ln_b.reshape(1, -1), b_dw, c_bd, c_scale.reshape(1, -1), w_out, x, gain.reshape(1, -1))


def _block_diag(c_w):
    g, n, _ = c_w.shape
    eye = jnp.eye(g, dtype=c_w.dtype)
    return (eye[:, None, :, None] * c_w[:, :, None, :]).reshape(g * n, g * n)


def kernel(x_prompt, x_sample, state_conv_a, state_conv_b, state_pool, norm_mix, w_in, b_in,
           a_dw, a_dw_b, a_ln_g, a_ln_b, b_dw, c_w, c_scale, w_out, norm_ffn, w_up, w_down,
           norm_final):
    depth = w_in.shape[0]
    bp, tp, d_model = x_prompt.shape
    bs, ts, _ = x_sample.shape
    assert bp == 1, "prompt rows must form one stream"
    n_prompt = bp * tp
    n_sample = bs * ts
    bf16 = jnp.bfloat16
    f32 = jnp.float32

    w_in_b = w_in.astype(bf16)
    w_out_b = w_out.astype(bf16)
    w_up_b = w_up.astype(bf16)
    w_down_b = w_down.astype(bf16)

    tm = 1536
    assert (n_prompt + n_sample) % tm == 0

    x, xg, ssq = _embed(x_prompt.reshape(n_prompt, d_model).astype(f32),
                        x_sample.reshape(n_sample, d_model).astype(f32), norm_mix[0])
    new_p = ([], [], [])
    new_s = ([], [], [])
    for l in range(depth):
        proj = _matmul(xg, w_in_b, l, tm=tm, tn=1024, ssq=ssq, bias=b_in[l], out_dtype=f32,
                       name="in_proj")
        c_bd = _block_diag(c_w[l]).astype(bf16)
        x, hg, ssq, na_p, nb_p, nc_p, na_s, nb_s, nc_s = _mix_out(
            proj, x, state_conv_a[l].astype(f32), state_conv_b[l].astype(f32),
            state_pool[l].astype(f32), a_dw[l], a_dw_b[l], a_ln_g[l], a_ln_b[l], b_dw[l], c_bd,
            c_scale[l], w_out_b, l, norm_ffn[l], n_prompt=n_prompt, sample_rows=ts)
        u = _matmul(hg, w_up_b, l, tm=tm, tn=1024, ssq=ssq, act="relu2", out_dtype=bf16,
                    name="ffn_up")
        if l + 1 < depth:
            x, xg, ssq = _matmul(u, w_down_b, l, tm=tm, tn=1024, tk=2048, res=x,
                                 gain=norm_mix[l + 1], emit_ssq=True, out_dtype=f32,
                                 name="ffn_down")
        else:
            x, ssq = _matmul(u, w_down_b, l, tm=tm, tn=1024, tk=2048, res=x, emit_ssq=True,
                             out_dtype=f32, name="ffn_down_last")
        for dst, val in zip(new_p + new_s, (na_p, nb_p, nc_p, na_s, nb_s, nc_s)):
            dst.append(val)

    y_prompt, y_sample = _final_norm(x, ssq, norm_final, n_prompt=n_prompt,
                                     out_dtype=x_prompt.dtype)
    stack = lambda parts: jnp.stack(parts, 0)
    return (y_prompt.reshape(bp, tp, d_model), y_sample.reshape(bs, ts, d_model),
            stack(new_p[0]), stack(new_p[1]), stack(new_p[2]),
            stack(new_s[0]), stack(new_s[1]), stack(new_s[2]))
```

```python
import functools

import jax
import jax.numpy as jnp
from jax import lax
from jax.experimental import pallas as pl
from jax.experimental.pallas import tpu as pltpu

EPS = 1e-6
PAST_LEN = 4096
POOL_WINDOWS = (2, 4, 8, 16)
CONV_A_TAPS = 31
CONV_B_TAPS = 3
HIST_A = CONV_A_TAPS - 1
HIST_B = CONV_B_TAPS - 1
HIST_C = max(POOL_WINDOWS) - 1

V7X_VMEM_BYTES = 64 * 1024 * 1024
V7X_SUBLANES = 8
V7X_LANES = 128
VMEM_LIMIT_BYTES = V7X_VMEM_BYTES - 4 * 1024 * 1024

EXT_A_PAD = 32
EXT_B_PAD = 8
EXT_C_PAD = 16
MIXER_ROW_CHUNK = 32
CONV_A_ROW_CHUNK = 64

assert all(w & (w - 1) == 0 for w in POOL_WINDOWS) and list(POOL_WINDOWS) == sorted(POOL_WINDOWS)


def _compiler_params(semantics):
    return pltpu.CompilerParams(dimension_semantics=semantics,
                                vmem_limit_bytes=VMEM_LIMIT_BYTES)


def _embed_kernel(xp_ref, xs_ref, g_ref, x_ref, xg_ref, ssq_ref, *, n_prompt_blocks, chunk):
    i = pl.program_id(0)
    rows = x_ref.shape[0]
    g = g_ref[...]

    def copy_from(src_ref):
        def body(c, carry):
            rs = pl.ds(pl.multiple_of(c * chunk, chunk), chunk)
            x = src_ref[rs, :]
            x_ref[rs, :] = x
            xg_ref[rs, :] = (x * g).astype(xg_ref.dtype)
            ssq_ref[rs, :] = jnp.sum(x * x, axis=-1, keepdims=True)
            return carry
        lax.fori_loop(0, rows // chunk, body, 0)

    pl.when(i < n_prompt_blocks)(lambda: copy_from(xp_ref))
    pl.when(i >= n_prompt_blocks)(lambda: copy_from(xs_ref))


def _embed(x_prompt, x_sample, g, *, chunk=16):
    n_prompt, d = x_prompt.shape
    block_rows = x_sample.shape[0]
    assert n_prompt % block_rows == 0 and block_rows % chunk == 0
    n_prompt_blocks = n_prompt // block_rows
    n = n_prompt + block_rows
    row_block = lambda i: (i, 0)
    return pl.pallas_call(
        functools.partial(_embed_kernel, n_prompt_blocks=n_prompt_blocks, chunk=chunk),
        out_shape=(jax.ShapeDtypeStruct((n, d), jnp.float32),
                   jax.ShapeDtypeStruct((n, d), jnp.bfloat16),
                   jax.ShapeDtypeStruct((n, 1), jnp.float32)),
        grid=(n_prompt_blocks + 1,),
        in_specs=[pl.BlockSpec((block_rows, d), lambda i: (jnp.minimum(i, n_prompt_blocks - 1), 0)),
                  pl.BlockSpec((block_rows, d), lambda i: (0, 0)),
                  pl.BlockSpec((1, d), lambda i: (0, 0))],
        out_specs=(pl.BlockSpec((block_rows, d), row_block),
                   pl.BlockSpec((block_rows, d), row_block),
                   pl.BlockSpec((block_rows, 1), row_block)),
        compiler_params=_compiler_params(("arbitrary",)),
        name="embed",
    )(x_prompt, x_sample, g.reshape(1, d))


def _final_norm_kernel(x_ref, ssq_ref, g_ref, yp_ref, ys_ref, *, n_prompt_blocks, chunk):
    i = pl.program_id(0)
    rows, d = x_ref.shape
    g = g_ref[...]

    def write_to(dst_ref):
        def body(c, carry):
            rs = pl.ds(pl.multiple_of(c * chunk, chunk), chunk)
            r = lax.rsqrt(ssq_ref[rs, :] * (1.0 / d) + EPS)
            dst_ref[rs, :] = (x_ref[rs, :] * r * g).astype(dst_ref.dtype)
            return carry
        lax.fori_loop(0, rows // chunk, body, 0)

    pl.when(i < n_prompt_blocks)(lambda: write_to(yp_ref))
    pl.when(i >= n_prompt_blocks)(lambda: write_to(ys_ref))


def _final_norm(x, ssq, g, *, n_prompt, out_dtype, chunk=16):
    n, d = x.shape
    block_rows = n - n_prompt
    assert n_prompt % block_rows == 0 and block_rows % chunk == 0
    n_prompt_blocks = n_prompt // block_rows
    row_block = lambda i: (i, 0)
    return pl.pallas_call(
        functools.partial(_final_norm_kernel, n_prompt_blocks=n_prompt_blocks, chunk=chunk),
        out_shape=(jax.ShapeDtypeStruct((n_prompt, d), out_dtype),
                   jax.ShapeDtypeStruct((block_rows, d), out_dtype)),
        grid=(n_prompt_blocks + 1,),
        in_specs=[pl.BlockSpec((block_rows, d), row_block),
                  pl.BlockSpec((block_rows, 1), row_block),
                  pl.BlockSpec((1, d), lambda i: (0, 0))],
        out_specs=(pl.BlockSpec((block_rows, d), lambda i: (jnp.minimum(i, n_prompt_blocks - 1), 0)),
                   pl.BlockSpec((block_rows, d), lambda i: (0, 0))),
        compiler_params=_compiler_params(("arbitrary",)),
        name="final_norm",
    )(x, ssq, g.reshape(1, d))


def _matmul_kernel(*refs, nk, norm_dim, has_bias, act, has_res, emit_xg, emit_ssq, sub_rows):
    it = iter(refs)
    a_ref = next(it)
    b_ref = next(it)
    ssq_in_ref = next(it) if norm_dim else None
    bias_ref = next(it) if has_bias else None
    res_ref = next(it) if has_res else None
    gain_ref = next(it) if emit_xg else None
    o_ref = next(it)
    xg_ref = next(it) if emit_xg else None
    ssq_out_ref = next(it) if emit_ssq else None
    acc_ref = next(it) if nk > 1 else None

    j = pl.program_id(1)
    k = pl.program_id(2)
    tm = a_ref.shape[0]
    row_slices = [slice(r * sub_rows, (r + 1) * sub_rows) for r in range(tm // sub_rows)]

    if emit_ssq:
        @pl.when((j == 0) & (k == 0))
        def _():
            ssq_out_ref[...] = jnp.zeros_like(ssq_out_ref)

    def epilogue(acc, rs):
        if norm_dim:
            acc = acc * lax.rsqrt(ssq_in_ref[rs, :] * (1.0 / norm_dim) + EPS)
        if has_bias:
            acc = acc + bias_ref[...]
        if act == "relu2":
            r = jnp.maximum(acc, 0.0)
            acc = r * r
        if has_res:
            acc = acc + res_ref[rs, :]
        o_ref[rs, :] = acc.astype(o_ref.dtype)
        if emit_xg:
            xg_ref[rs, :] = (acc * gain_ref[...]).astype(xg_ref.dtype)
        if emit_ssq:
            ssq_out_ref[rs, :] += jnp.sum(acc * acc, axis=-1, keepdims=True)

    def run(first, last):
        for rs in row_slices:
            prod = jnp.dot(a_ref[rs, :], b_ref[...], preferred_element_type=jnp.float32)
            if not first:
                prod = prod + acc_ref[rs, :]
            if last:
                epilogue(prod, rs)
            else:
                acc_ref[rs, :] = prod

    if nk == 1:
        run(True, True)
    else:
        pl.when(k == 0)(lambda: run(True, False))
        pl.when((k > 0) & (k < nk - 1))(lambda: run(False, False))
        pl.when(k == nk - 1)(lambda: run(False, True))


def _matmul(a, b, layer, *, tm, tn, tk=None, ssq=None, bias=None, act=None, res=None, gain=None,
            emit_ssq=False, out_dtype, name, sub_rows=256):
    m, kdim = a.shape
    _, _, n = b.shape
    tk = kdim if tk is None else tk
    assert m % tm == 0 and n % tn == 0 and kdim % tk == 0 and tm % sub_rows == 0
    nk = kdim // tk
    assert nk == 1 or nk >= 3
    in_specs = [pl.BlockSpec((tm, tk), lambda i, j, k: (i, k)),
                pl.BlockSpec((None, tk, tn), lambda i, j, k: (layer, k, j))]
    operands = [a, b]
    if ssq is not None:
        in_specs.append(pl.BlockSpec((tm, 1), lambda i, j, k: (i, 0)))
        operands.append(ssq)
    if bias is not None:
        in_specs.append(pl.BlockSpec((1, tn), lambda i, j, k: (0, j)))
        operands.append(bias.reshape(1, n))
    if res is not None:
        in_specs.append(pl.BlockSpec((tm, tn), lambda i, j, k: (i, j)))
        operands.append(res)
    out_shape = [jax.ShapeDtypeStruct((m, n), out_dtype)]
    out_specs = [pl.BlockSpec((tm, tn), lambda i, j, k: (i, j))]
    if gain is not None:
        in_specs.append(pl.BlockSpec((1, tn), lambda i, j, k: (0, j)))
        operands.append(gain.reshape(1, n))
        out_shape.append(jax.ShapeDtypeStruct((m, n), jnp.bfloat16))
        out_specs.append(pl.BlockSpec((tm, tn), lambda i, j, k: (i, j)))
    if emit_ssq:
        out_shape.append(jax.ShapeDtypeStruct((m, 1), jnp.float32))
        out_specs.append(pl.BlockSpec((tm, 1), lambda i, j, k: (i, 0)))
    scratch = [pltpu.VMEM((tm, tn), jnp.float32)] if nk > 1 else []
    out = pl.pallas_call(
        functools.partial(_matmul_kernel, nk=nk, norm_dim=kdim if ssq is not None else 0,
                          has_bias=bias is not None, act=act, has_res=res is not None,
                          emit_xg=gain is not None, emit_ssq=emit_ssq, sub_rows=sub_rows),
        out_shape=tuple(out_shape),
        grid=(m // tm, n // tn, nk),
        in_specs=in_specs,
        out_specs=tuple(out_specs),
        scratch_shapes=scratch,
        compiler_params=_compiler_params(("arbitrary", "arbitrary", "arbitrary")),
        name=name,
    )(*operands)
    return out if len(out) > 1 else out[0]


def _mixer_rows(proj_ref, prm, scr, mix_scr, mix_row0, row0, rows, pos0, widths):
    adw_ref, adwb_ref, lng_ref, lnb_ref, bdw_ref = prm
    ext_a, ext_b, ext_c, conv_a, pool_c = scr
    a_width, b_width, c_width = widths
    rc = MIXER_ROW_CHUNK
    off_gate = a_width
    off_bb = 2 * a_width
    off_bc = off_bb + b_width
    off_bh = off_bc + b_width
    off_cu = off_bh + b_width
    pool_group = c_width // len(POOL_WINDOWS)
    lane = lax.broadcasted_iota(jnp.int32, (rc, V7X_LANES), 1)
    row = lax.broadcasted_iota(jnp.int32, (rc, 1), 0)

    def shifted(win, offset):
        sub = offset % V7X_SUBLANES
        base = offset - sub
        if sub:
            win = pltpu.roll(win, win.shape[0] - sub, axis=0)
        return win[base:base + rc]

    for c in range(rows // rc):
        rs = slice(row0 + c * rc, row0 + (c + 1) * rc)
        a_val = proj_ref[rs, 0:a_width]
        a_gate = proj_ref[rs, off_gate:off_gate + a_width]
        ext_a[EXT_A_PAD + c * rc:EXT_A_PAD + (c + 1) * rc, :] = a_val * jax.nn.sigmoid(a_gate)
        b_c = proj_ref[rs, off_bc:off_bc + b_width]
        b_h = proj_ref[rs, off_bh:off_bh + b_width]
        ext_b[EXT_B_PAD + c * rc:EXT_B_PAD + (c + 1) * rc, :] = b_c * b_h
        ext_c[EXT_C_PAD + c * rc:EXT_C_PAD + (c + 1) * rc, :] = proj_ref[rs, off_cu:off_cu + c_width]
        yield

    cc = min(rows, CONV_A_ROW_CHUNK)
    for c in range(rows // cc):
        for lc in range(a_width // V7X_LANES):
            ls = slice(lc * V7X_LANES, (lc + 1) * V7X_LANES)
            win = ext_a[c * cc:(c + 1) * cc + EXT_A_PAD, ls]
            acc = jnp.broadcast_to(adwb_ref[:, ls], (cc, V7X_LANES))
            for sub in range(V7X_SUBLANES):
                rows_p = cc + V7X_SUBLANES if sub else cc
                part = None
                for t in range(CONV_A_TAPS):
                    offset = EXT_A_PAD - HIST_A + t
                    if offset % V7X_SUBLANES == sub:
                        base = offset - sub
                        term = win[base:base + rows_p] * adw_ref[t:t + 1, ls]
                        part = term if part is None else part + term
                if sub:
                    part = pltpu.roll(part, rows_p - sub, axis=0)[:cc]
                acc = acc + part
            conv_a[row0 + c * cc:row0 + (c + 1) * cc, ls] = acc
            yield

    for c in range(rows // rc):
        rs = slice(row0 + c * rc, row0 + (c + 1) * rc)
        dst = pl.ds(mix_row0 + (row0 + c * rc), rc)
        x = conv_a[rs, :]
        mu = jnp.mean(x, axis=-1, keepdims=True)
        xc = x - mu
        var = jnp.mean(xc * xc, axis=-1, keepdims=True)
        y = xc * lax.rsqrt(var + EPS) * lng_ref[...] + lnb_ref[...]
        mix_scr[dst, 0:a_width] = (y * jax.nn.sigmoid(y)).astype(mix_scr.dtype)

        win_b = ext_b[c * rc:(c + 1) * rc + EXT_B_PAD, :]
        conv_b = shifted(win_b, EXT_B_PAD - HIST_B) * bdw_ref[0:1, :]
        for t in range(1, CONV_B_TAPS):
            conv_b = conv_b + shifted(win_b, EXT_B_PAD - HIST_B + t) * bdw_ref[t:t + 1, :]
        b_b = proj_ref[rs, off_bb:off_bb + b_width]
        mix_scr[dst, a_width:a_width + b_width] = (b_b * conv_b).astype(mix_scr.dtype)

        win_c = ext_c[c * rc:(c + 1) * rc + EXT_C_PAD, :]
        cur = win_c[EXT_C_PAD:]
        pos = pos0 + c * rc + row
        wsum, col0, span = win_c, 0, 1
        means = []
        for g, w in enumerate(POOL_WINDOWS):
            start = (g * pool_group) // V7X_LANES * V7X_LANES
            wsum, col0 = wsum[:, start - col0:], start
            while span < w:
                wsum = wsum + pltpu.roll(wsum, span, axis=0)
                span *= 2
            inv_cnt = 1.0 / jnp.minimum(pos + 1, w).astype(jnp.float32)
            means.append((wsum[EXT_C_PAD:] * inv_cnt, col0))
        tiles = []
        for lo in range(0, c_width, V7X_LANES):
            g_lo = lo // pool_group
            g_hi = (lo + V7X_LANES - 1) // pool_group
            tile_of = lambda g: means[g][0][:, lo - means[g][1]:lo - means[g][1] + V7X_LANES]
            if g_lo == g_hi:
                tiles.append(tile_of(g_lo))
            else:
                tiles.append(jnp.where(lane < g_hi * pool_group - lo, tile_of(g_lo),
                                       tile_of(g_hi)))
        pooled = jnp.concatenate(tiles, axis=-1)
        pool_c[rs, :] = (pooled - cur).astype(pool_c.dtype)
        yield


def _mix_out_kernel(proj_ref, sa_ref, sb_ref, sc_ref, adw_ref, adwb_ref, lng_ref, lnb_ref,
                    bdw_ref, cbd_ref, cs_ref, w_ref, res_ref, gain_ref,
                    o_ref, xg_ref, ssq_ref, nap_ref, nbp_ref, ncp_ref, nas_ref, nbs_ref, ncs_ref,
                    mix_even, mix_odd, ext_a, ext_b, ext_c, conv_a, pool_c,
                    *, n_prompt_tiles, sample_chunks, sample_rows, widths, sub_rows):
    i = pl.program_id(0)
    j = pl.program_id(1)
    a_width, b_width, c_width = widths
    tile_rows = o_ref.shape[0]
    chunk_rows = proj_ref.shape[0]
    prm = (adw_ref, adwb_ref, lng_ref, lnb_ref, bdw_ref)
    scr = (ext_a, ext_b, ext_c, conv_a, pool_c)
    mix_row0 = pl.multiple_of(j * chunk_rows, chunk_rows)

    def set_history(hist_a, hist_b, hist_c):
        ext_a[0:EXT_A_PAD, :] = jnp.zeros((EXT_A_PAD, a_width), jnp.float32)
        ext_b[0:EXT_B_PAD, :] = jnp.zeros((EXT_B_PAD, b_width), jnp.float32)
        ext_c[0:EXT_C_PAD, :] = jnp.zeros((EXT_C_PAD, c_width), jnp.float32)
        if hist_a is not None:
            ext_a[EXT_A_PAD - HIST_A:EXT_A_PAD, :] = hist_a
            ext_b[EXT_B_PAD - HIST_B:EXT_B_PAD, :] = hist_b
            ext_c[EXT_C_PAD - HIST_C:EXT_C_PAD, :] = hist_c

    def new_state(rows):
        return (ext_a[rows + EXT_A_PAD - HIST_A:rows + EXT_A_PAD, :],
                ext_b[rows + EXT_B_PAD - HIST_B:rows + EXT_B_PAD, :],
                ext_c[rows + EXT_C_PAD - HIST_C:rows + EXT_C_PAD, :])

    def pooled_map(mix_scr):
        c_lin = jnp.dot(pool_c[...], cbd_ref[...], preferred_element_type=jnp.float32)
        mix_scr[pl.ds(mix_row0, chunk_rows), a_width + b_width:] = (
            c_lin * cs_ref[...]).astype(mix_scr.dtype)

    def prompt_chunk(mix_scr):
        yield from _mixer_rows(proj_ref, prm, scr, mix_scr, mix_row0, 0, chunk_rows,
                               i * tile_rows + j * chunk_rows, widths)
        pooled_map(mix_scr)
        nap_ref[0], nbp_ref[0], ncp_ref[0] = new_state(chunk_rows)
        ext_a[0:EXT_A_PAD, :] = ext_a[chunk_rows:chunk_rows + EXT_A_PAD, :]
        ext_b[0:EXT_B_PAD, :] = ext_b[chunk_rows:chunk_rows + EXT_B_PAD, :]
        ext_c[0:EXT_C_PAD, :] = ext_c[chunk_rows:chunk_rows + EXT_C_PAD, :]
        yield

    def sample_chunk(mix_scr):
        for s in range(chunk_rows // sample_rows):
            set_history(sa_ref[s], sb_ref[s], sc_ref[s])
            yield from _mixer_rows(proj_ref, prm, scr, mix_scr, mix_row0, s * sample_rows,
                                   sample_rows, PAST_LEN, widths)
            nas_ref[s], nbs_ref[s], ncs_ref[s] = new_state(sample_rows)
        pooled_map(mix_scr)
        yield

    def project(mix_scr):
        for r in range(tile_rows // sub_rows):
            rs = slice(r * sub_rows, (r + 1) * sub_rows)
            acc = jnp.dot(mix_scr[rs, :], w_ref[...], preferred_element_type=jnp.float32)
            acc = acc + res_ref[rs, :]
            o_ref[rs, :] = acc
            xg_ref[rs, :] = (acc * gain_ref[...]).astype(xg_ref.dtype)
            ssq_ref[rs, :] += jnp.sum(acc * acc, axis=-1, keepdims=True)
            yield

    def run(matmul_units, mixer_units, mixers_per_matmul):
        for _ in matmul_units or ():
            for _ in range(mixers_per_matmul):
                if mixer_units is None or next(mixer_units, "done") == "done":
                    break
        for _ in mixer_units or ():
            pass

    def units_per_matmul(rows, segments):
        per_segment = (2 * (rows // MIXER_ROW_CHUNK)
                       + (rows // min(rows, CONV_A_ROW_CHUNK)) * (a_width // V7X_LANES))
        return -(-(segments * per_segment + 1) // (tile_rows // sub_rows))

    prompt_rate = units_per_matmul(chunk_rows, 1)
    sample_rate = units_per_matmul(sample_rows, chunk_rows // sample_rows)

    @pl.when((i >= 1) & (j == 0))
    def _():
        ssq_ref[...] = jnp.zeros_like(ssq_ref)

    is_even = i % 2 == 0
    in_prompt = i < n_prompt_tiles
    sample_tile = n_prompt_tiles

    @pl.when(i == 0)
    def _():
        pl.when(j == 0)(lambda: set_history(None, None, None))
        run(None, prompt_chunk(mix_even), 0)

    @pl.when((i >= 1) & in_prompt & is_even)
    def _():
        run(project(mix_odd), prompt_chunk(mix_even), prompt_rate)

    @pl.when(in_prompt & jnp.logical_not(is_even))
    def _():
        run(project(mix_even), prompt_chunk(mix_odd), prompt_rate)

    @pl.when((i == sample_tile) & (j < sample_chunks))
    def _():
        run(project(mix_odd), sample_chunk(mix_even), sample_rate)

    @pl.when((i == sample_tile) & (j >= sample_chunks))
    def _():
        run(project(mix_odd), None, 0)

    @pl.when(i == sample_tile + 1)
    def _():
        run(project(mix_even), None, 0)


def _mix_out(proj, x, state_a, state_b, state_c, a_dw, a_dw_b, a_ln_g, a_ln_b, b_dw, c_bd,
             c_scale, w_out, layer, gain, *, n_prompt, sample_rows, tile_rows=1024, tn=512,
             sub_rows=512):
    n_rows, in_cols = proj.shape
    d_model = x.shape[1]
    a_width = a_dw.shape[1]
    b_width = b_dw.shape[1]
    c_width = c_scale.shape[0]
    widths = (a_width, b_width, c_width)
    mix_width = a_width + b_width + c_width
    n_streams = state_a.shape[0]
    n_col = d_model // tn
    chunk_rows = tile_rows // n_col
    streams_per_chunk = chunk_rows // sample_rows
    n_prompt_tiles = n_prompt // tile_rows
    n_sample = n_rows - n_prompt
    sample_chunks = n_sample // chunk_rows
    assert d_model % tn == 0 and tile_rows % n_col == 0 and tile_rows % sub_rows == 0
    assert n_prompt % tile_rows == 0 and n_prompt_tiles % 2 == 0
    assert chunk_rows % sample_rows == 0 and chunk_rows % CONV_A_ROW_CHUNK == 0
    assert sample_rows % MIXER_ROW_CHUNK == 0 and sample_rows >= HIST_A
    assert n_sample == n_streams * sample_rows and n_sample % chunk_rows == 0
    assert 0 < sample_chunks <= n_col and mix_width == w_out.shape[1]
    last_chunk = n_rows // chunk_rows - 1
    first_sample_chunk = n_prompt // chunk_rows

    def const(shape):
        return pl.BlockSpec(shape, lambda i, j: tuple(0 for _ in shape))

    def sample_state(h, w):
        def index(i, j):
            c = jnp.clip(i * n_col + j - first_sample_chunk, 0, sample_chunks - 1)
            return (c, 0, 0)
        return pl.BlockSpec((streams_per_chunk, h, w), index)

    def out_block(i, j):
        return (jnp.maximum(i - 1, 0), jnp.where(i == 0, 0, j))

    f32 = jnp.float32
    kern = functools.partial(_mix_out_kernel, n_prompt_tiles=n_prompt_tiles,
                             sample_chunks=sample_chunks, sample_rows=sample_rows, widths=widths,
                             sub_rows=sub_rows)
    return pl.pallas_call(
        kern,
        out_shape=(jax.ShapeDtypeStruct((n_rows, d_model), f32),
                   jax.ShapeDtypeStruct((n_rows, d_model), jnp.bfloat16),
                   jax.ShapeDtypeStruct((n_rows, 1), f32),
                   jax.ShapeDtypeStruct((1, HIST_A, a_width), f32),
                   jax.ShapeDtypeStruct((1, HIST_B, b_width), f32),
                   jax.ShapeDtypeStruct((1, HIST_C, c_width), f32),
                   jax.ShapeDtypeStruct((n_streams, HIST_A, a_width), f32),
                   jax.ShapeDtypeStruct((n_streams, HIST_B, b_width), f32),
                   jax.ShapeDtypeStruct((n_streams, HIST_C, c_width), f32)),
        grid=(n_prompt_tiles + 2, n_col),
        in_specs=[pl.BlockSpec((chunk_rows, in_cols),
                               lambda i, j: (jnp.minimum(i * n_col + j, last_chunk), 0)),
                  sample_state(HIST_A, a_width),
                  sample_state(HIST_B, b_width),
                  sample_state(HIST_C, c_width),
                  const((CONV_A_TAPS, a_width)),
                  const((1, a_width)), const((1, a_width)), const((1, a_width)),
                  const((CONV_B_TAPS, b_width)),
                  const((c_width, c_width)),
                  const((1, c_width)),
                  pl.BlockSpec((None, mix_width, tn), lambda i, j: (layer, 0, j)),
                  pl.BlockSpec((tile_rows, tn), out_block),
                  pl.BlockSpec((1, tn), lambda i, j: (0, j))],
        out_specs=(pl.BlockSpec((tile_rows, tn), out_block),
                   pl.BlockSpec((tile_rows, tn), out_block),
                   pl.BlockSpec((tile_rows, 1), lambda i, j: (jnp.maximum(i - 1, 0), 0)),
                   const((1, HIST_A, a_width)),
                   const((1, HIST_B, b_width)),
                   const((1, HIST_C, c_width)),
                   sample_state(HIST_A, a_width),
                   sample_state(HIST_B, b_width),
                   sample_state(HIST_C, c_width)),
        scratch_shapes=[pltpu.VMEM((tile_rows, mix_width), jnp.bfloat16),
                        pltpu.VMEM((tile_rows, mix_width), jnp.bfloat16),
                        pltpu.VMEM((chunk_rows + EXT_A_PAD, a_width), f32),
                        pltpu.VMEM((chunk_rows + EXT_B_PAD, b_width), f32),
                        pltpu.VMEM((chunk_rows + EXT_C_PAD, c_width), f32),
                        pltpu.VMEM((chunk_rows, a_width), f32),
                        pltpu.VMEM((chunk_rows, c_width), jnp.bfloat16)],
        compiler_params=_compiler_params(("arbitrary", "arbitrary")),
        name="mix_out",
    )(proj, state_a, state_b, state_c, a_dw, a_dw_b.reshape(1, -1), a_ln_g.reshape(1, -1),
      a_ln_b.reshape(1, -1), b_dw, c_bd, c_scale.reshape(1, -1), w_out, x, gain.reshape(1, -1))


def _block_diag(c_w):
    g, n, _ = c_w.shape
    eye = jnp.eye(g, dtype=c_w.dtype)
    return (eye[:, None, :, None] * c_w[:, :, None, :]).reshape(g * n, g * n)


def kernel(x_prompt, x_sample, state_conv_a, state_conv_b, state_pool, norm_mix, w_in, b_in,
           a_dw, a_dw_b, a_ln_g, a_ln_b, b_dw, c_w, c_scale, w_out, norm_ffn, w_up, w_down,
           norm_final):
    depth = w_in.shape[0]
    bp, tp, d_model = x_prompt.shape
    bs, ts, _ = x_sample.shape
    assert bp == 1, "prompt rows must form one stream"
    n_prompt = bp * tp
    n_sample = bs * ts
    bf16 = jnp.bfloat16
    f32 = jnp.float32

    w_in_b = w_in.astype(bf16)
    w_out_b = w_out.astype(bf16)
    w_up_b = w_up.astype(bf16)
    w_down_b = w_down.astype(bf16)

    tm = 1536
    assert (n_prompt + n_sample) % tm == 0

    x, xg, ssq = _embed(x_prompt.reshape(n_prompt, d_model).astype(f32),
                        x_sample.reshape(n_sample, d_model).astype(f32), norm_mix[0])
    new_p = ([], [], [])
    new_s = ([], [], [])
    for l in range(depth):
        proj = _matmul(xg, w_in_b, l, tm=tm, tn=1024, ssq=ssq, bias=b_in[l], out_dtype=f32,
                       name="in_proj")
        c_bd = _block_diag(c_w[l]).astype(bf16)
        x, hg, ssq, na_p, nb_p, nc_p, na_s, nb_s, nc_s = _mix_out(
            proj, x, state_conv_a[l].astype(f32), state_conv_b[l].astype(f32),
            state_pool[l].astype(f32), a_dw[l], a_dw_b[l], a_ln_g[l], a_ln_b[l], b_dw[l], c_bd,
            c_scale[l], w_out_b, l, norm_ffn[l], n_prompt=n_prompt, sample_rows=ts)
        u = _matmul(hg, w_up_b, l, tm=tm, tn=1024, ssq=ssq, act="relu2", out_dtype=bf16,
                    name="ffn_up")
        if l + 1 < depth:
            x, xg, ssq = _matmul(u, w_down_b, l, tm=tm, tn=1024, tk=2048, res=x,
                                 gain=norm_mix[l + 1], emit_ssq=True, out_dtype=f32,
                                 name="ffn_down")
        else:
            x, ssq = _matmul(u, w_down_b, l, tm=tm, tn=1024, tk=2048, res=x, emit_ssq=True,
                             out_dtype=f32, name="ffn_down_last")
        for dst, val in zip(new_p + new_s, (na_p, nb_p, nc_p, na_s, nb_s, nc_s)):
            dst.append(val)

    y_prompt, y_sample = _final_norm(x, ssq, norm_final, n_prompt=n_prompt,
                                     out_dtype=x_prompt.dtype)
    stack = lambda parts: jnp.stack(parts, 0)
    return (y_prompt.reshape(bp, tp, d_model), y_sample.reshape(bs, ts, d_model),
            stack(new_p[0]), stack(new_p[1]), stack(new_p[2]),
            stack(new_s[0]), stack(new_s[1]), stack(new_s[2]))
```
